```python
import math
import jax, jax.numpy as jnp
from jax import lax
import numpy as np

D_MODEL = 4096
BATCH = 4
SEQ = 2048
DEPTH = 4
DEC_BATCH = 8
DEC_SEQ = 8
PAST_LEN = 8192
PAGE_SIZE = 128

N_MIXERS = 2
N_S5_LAYERS = (DEPTH + N_MIXERS - 1) // N_MIXERS
N_ATTN_LAYERS = DEPTH // N_MIXERS
BRANCH = D_MODEL
S5_GROUP = 16
S5_GROUPS = BRANCH // S5_GROUP
S5_STATE = 64
HEAD_DIM = 128
VALUE_DIM = 2 * HEAD_DIM
N_HEADS = BRANCH // VALUE_DIM
Q_BLOCK = 128
EPS = 1e-6
STEP_MIN = 0.001
STEP_MAX = 0.1
ATTN_SCALE = 1.0 / math.sqrt(HEAD_DIM)

kernel_name = "hybrid_s5_diffattn_decode_step"

F32 = jnp.float32


def rmsnorm(x, g):
    xf = x.astype(F32)
    y = xf * lax.rsqrt(jnp.mean(xf * xf, axis=-1, keepdims=True) + EPS)
    return (y * g.astype(F32)).astype(x.dtype)


def lambda_init_fn(layer):
    return 0.8 - 0.6 * math.exp(-0.3 * layer)


def s5_discretize(lam_re, lam_im, log_step, b_re, b_im):
    step = jnp.exp(log_step.astype(F32))[:, None]
    lr, li = lam_re.astype(F32), lam_im.astype(F32)
    mag = jnp.exp(lr * step)
    ar, ai = mag * jnp.cos(li * step), mag * jnp.sin(li * step)
    den = lr * lr + li * li
    nr, ni = ar - 1.0, ai
    fr = (nr * lr + ni * li) / den
    fi = (ni * lr - nr * li) / den
    br, bi = b_re.astype(F32), b_im.astype(F32)
    bbr = fr[..., None] * br - fi[..., None] * bi
    bbi = fr[..., None] * bi + fi[..., None] * br
    return ar, ai, bbr, bbi


def s5_scan(u, h0r, h0i, ar, ai, bbr, bbi, c_re, c_im, d):
    bsz, L, _ = u.shape
    ug = u.astype(F32).reshape(bsz, L, S5_GROUPS, S5_GROUP)
    bur = jnp.einsum('blgc,gpc->lbgp', ug, bbr)
    bui = jnp.einsum('blgc,gpc->lbgp', ug, bbi)
    a_r = jnp.broadcast_to(ar[None, None], (L, 1, S5_GROUPS, S5_STATE))
    a_i = jnp.broadcast_to(ai[None, None], (L, 1, S5_GROUPS, S5_STATE))

    def combine(e1, e2):
        a1r, a1i, b1r, b1i = e1
        a2r, a2i, b2r, b2i = e2
        return (a2r * a1r - a2i * a1i,
                a2r * a1i + a2i * a1r,
                a2r * b1r - a2i * b1i + b2r,
                a2r * b1i + a2i * b1r + b2i)

    pr, pi, hr, hi = lax.associative_scan(combine, (a_r, a_i, bur, bui), axis=0)
    if h0r is not None:
        h0r = h0r.astype(F32)[None]
        h0i = h0i.astype(F32)[None]
        hr = hr + pr * h0r - pi * h0i
        hi = hi + pr * h0i + pi * h0r
    y = (jnp.einsum('lbgp,gcp->blgc', hr, c_re.astype(F32))
         - jnp.einsum('lbgp,gcp->blgc', hi, c_im.astype(F32)))
    y = y + d.astype(F32) * ug
    return y.reshape(bsz, L, BRANCH), hr[-1], hi[-1]


def s5_branch(x, h0r, h0i, disc, w_in, c_re, c_im, d, w_glu, b_glu, w_out):
    uz = x @ w_in
    u, z = jnp.split(uz, 2, axis=-1)
    ar, ai, bbr, bbi = disc
    y, hr, hi = s5_scan(u, h0r, h0i, ar, ai, bbr, bbi, c_re, c_im, d)
    g = jax.nn.gelu(y)
    g = g * jax.nn.sigmoid(g @ w_glu.astype(F32) + b_glu.astype(F32))
    out = (g * jax.nn.silu(z.astype(F32))).astype(x.dtype) @ w_out
    return out, hr, hi


def causal_diff_attention(q, k, v, lam, n_past):
    bsz, L = q.shape[:2]
    n_keys = k.shape[1]
    k5 = k.reshape(bsz, n_keys, N_HEADS, 2, HEAD_DIM)
    vf = v.astype(F32)
    key_pos = jnp.arange(n_keys)
    qb = Q_BLOCK if L % Q_BLOCK == 0 else L
    nb = L // qb
    q_blocks = q.reshape(bsz, nb, qb, N_HEADS, 2, HEAD_DIM).swapaxes(0, 1)
    q_pos = (n_past + jnp.arange(L)).reshape(nb, qb)

    def one_block(args):
        qblk, qpos = args
        s = jnp.einsum('bqhcd,bkhcd->bhcqk', qblk, k5, preferred_element_type=F32) * ATTN_SCALE
        s = jnp.where(key_pos[None, :] <= qpos[:, None], s, -jnp.inf)
        p = jax.nn.softmax(s, axis=-1)
        w = p[:, :, 0] - lam * p[:, :, 1]
        return jnp.einsum('bhqk,bkhe->bqhe', w, vf)

    o = lax.map(one_block, (q_blocks, q_pos))
    return o.swapaxes(0, 1).reshape(bsz, L, N_HEADS, VALUE_DIM)


def diff_attn_branch(x, k_past, v_past, lam, lambda_init, w_in, subln_g, w_out):
    bsz, L, _ = x.shape
    q, k, v, z = jnp.split(x @ w_in, 4, axis=-1)
    q = q.reshape(bsz, L, N_HEADS, 2, HEAD_DIM)
    k = k.reshape(bsz, L, N_HEADS, 2 * HEAD_DIM)
    v = v.reshape(bsz, L, N_HEADS, VALUE_DIM)
    if k_past is None:
        k_all, v_all, n_past = k, v, 0
    else:
        k_all = jnp.concatenate([k_past.astype(k.dtype), k], axis=1)
        v_all = jnp.concatenate([v_past.astype(v.dtype), v], axis=1)
        n_past = k_past.shape[1]
    o = causal_diff_attention(q, k_all, v_all, lam, n_past)
    o = rmsnorm(o, subln_g) * (1.0 - lambda_init)
    o = o.reshape(bsz, L, BRANCH) * jax.nn.silu(z.astype(F32))
    return o.astype(x.dtype) @ w_out, k, v


def setup_inputs(seed: int = 0) -> dict:
    key = jax.random.key(seed)
    ks = jax.random.split(key, 32)
    n_pages = PAST_LEN // PAGE_SIZE
    n_used = DEC_BATCH * n_pages
    n_pool = n_used + n_used // 4

    def nrm(k, shape, s):
        return s * jax.random.normal(k, shape, F32)

    x_prompt = nrm(ks[0], (BATCH, SEQ, D_MODEL), 1.0)
    x_sample = nrm(ks[1], (DEC_BATCH, DEC_SEQ, D_MODEL), 1.0)
    cache_k = nrm(ks[2], (N_ATTN_LAYERS, n_pool, PAGE_SIZE, N_HEADS, 2 * HEAD_DIM), 1.0)
    cache_v = nrm(ks[3], (N_ATTN_LAYERS, n_pool, PAGE_SIZE, N_HEADS, VALUE_DIM), 1.0)
    state_s5_re = nrm(ks[4], (N_S5_LAYERS, DEC_BATCH, S5_GROUPS, S5_STATE), 0.3)
    state_s5_im = nrm(ks[5], (N_S5_LAYERS, DEC_BATCH, S5_GROUPS, S5_STATE), 0.3)
    page_table = jax.random.permutation(ks[6], n_pool)[:n_used].reshape(DEC_BATCH, n_pages).astype(jnp.int32)

    norm_g = 1.0 + nrm(ks[7], (DEPTH, D_MODEL), 0.01)
    final_norm_g = 1.0 + nrm(ks[8], (D_MODEL,), 0.01)

    s5_w_in = nrm(ks[9], (N_S5_LAYERS, D_MODEL, 2 * BRANCH), D_MODEL ** -0.5)
    s5_lambda_re = -0.5 + nrm(ks[10], (N_S5_LAYERS, S5_GROUPS, S5_STATE), 0.01)
    s5_lambda_im = (jnp.pi * jnp.arange(S5_STATE, dtype=F32))[None, None, :] + nrm(ks[11], (N_S5_LAYERS, S5_GROUPS, S5_STATE), 0.01)
    s5_log_step = jax.random.uniform(ks[12], (N_S5_LAYERS, S5_GROUPS), F32, math.log(STEP_MIN), math.log(STEP_MAX))
    s5_b_re = nrm(ks[13], (N_S5_LAYERS, S5_GROUPS, S5_STATE, S5_GROUP), (2 * S5_GROUP) ** -0.5)
    s5_b_im = nrm(ks[14], (N_S5_LAYERS, S5_GROUPS, S5_STATE, S5_GROUP), (2 * S5_GROUP) ** -0.5)
    s5_c_re = nrm(ks[15], (N_S5_LAYERS, S5_GROUPS, S5_GROUP, S5_STATE), (2 * S5_STATE) ** -0.5)
    s5_c_im = nrm(ks[16], (N_S5_LAYERS, S5_GROUPS, S5_GROUP, S5_STATE), (2 * S5_STATE) ** -0.5)
    s5_d = nrm(ks[17], (N_S5_LAYERS, S5_GROUPS, S5_GROUP), 1.0)
    s5_w_glu = nrm(ks[18], (N_S5_LAYERS, BRANCH, BRANCH), BRANCH ** -0.5)
    s5_b_glu = nrm(ks[19], (N_S5_LAYERS, BRANCH), 0.01)
    s5_w_out = nrm(ks[20], (N_S5_LAYERS, BRANCH, D_MODEL), BRANCH ** -0.5)

    attn_w_in = nrm(ks[21], (N_ATTN_LAYERS, D_MODEL, 4 * BRANCH), D_MODEL ** -0.5)
    attn_lam_q1 = nrm(ks[22], (N_ATTN_LAYERS, HEAD_DIM), 0.1)
    attn_lam_k1 = nrm(ks[23], (N_ATTN_LAYERS, HEAD_DIM), 0.1)
    attn_lam_q2 = nrm(ks[24], (N_ATTN_LAYERS, HEAD_DIM), 0.1)
    attn_lam_k2 = nrm(ks[25], (N_ATTN_LAYERS, HEAD_DIM), 0.1)
    attn_subln_g = 1.0 + nrm(ks[26], (N_ATTN_LAYERS, VALUE_DIM), 0.01)
    attn_w_out = nrm(ks[27], (N_ATTN_LAYERS, BRANCH, D_MODEL), BRANCH ** -0.5)

    return {"x_prompt": x_prompt, "x_sample": x_sample, "cache_k": cache_k, "cache_v": cache_v,
            "state_s5_re": state_s5_re, "state_s5_im": state_s5_im, "page_table": page_table,
            "norm_g": norm_g, "final_norm_g": final_norm_g,
            "s5_w_in": s5_w_in, "s5_lambda_re": s5_lambda_re, "s5_lambda_im": s5_lambda_im,
            "s5_log_step": s5_log_step, "s5_b_re": s5_b_re, "s5_b_im": s5_b_im,
            "s5_c_re": s5_c_re, "s5_c_im": s5_c_im, "s5_d": s5_d, "s5_w_glu": s5_w_glu,
            "s5_b_glu": s5_b_glu, "s5_w_out": s5_w_out,
            "attn_w_in": attn_w_in, "attn_lam_q1": attn_lam_q1, "attn_lam_k1": attn_lam_k1,
            "attn_lam_q2": attn_lam_q2, "attn_lam_k2": attn_lam_k2, "attn_subln_g": attn_subln_g,
            "attn_w_out": attn_w_out}


def reference(x_prompt, x_sample, cache_k, cache_v, state_s5_re, state_s5_im, page_table,
              norm_g, final_norm_g,
              s5_w_in, s5_lambda_re, s5_lambda_im, s5_log_step, s5_b_re, s5_b_im,
              s5_c_re, s5_c_im, s5_d, s5_w_glu, s5_b_glu, s5_w_out,
              attn_w_in, attn_lam_q1, attn_lam_k1, attn_lam_q2, attn_lam_k2, attn_subln_g,
              attn_w_out):
    hp, hs = x_prompt, x_sample
    dec_b = x_sample.shape[0]
    past = page_table.shape[1] * PAGE_SIZE
    k_p, v_p, k_s, v_s = [], [], [], []
    sr_p, si_p, sr_s, si_s = [], [], [], []
    for layer in range(DEPTH):
        idx = layer // N_MIXERS
        xp = rmsnorm(hp, norm_g[layer])
        xs = rmsnorm(hs, norm_g[layer])
        if layer % N_MIXERS == 0:
            disc = s5_discretize(s5_lambda_re[idx], s5_lambda_im[idx], s5_log_step[idx],
                                 s5_b_re[idx], s5_b_im[idx])
            w = (s5_w_in[idx], s5_c_re[idx], s5_c_im[idx], s5_d[idx], s5_w_glu[idx],
                 s5_b_glu[idx], s5_w_out[idx])
            op, hr_p, hi_p = s5_branch(xp, None, None, disc, *w)
            os_, hr_s, hi_s = s5_branch(xs, state_s5_re[idx], state_s5_im[idx], disc, *w)
            sr_p.append(hr_p); si_p.append(hi_p); sr_s.append(hr_s); si_s.append(hi_s)
        else:
            lambda_init = lambda_init_fn(layer)
            lam = (jnp.exp(jnp.sum(attn_lam_q1[idx].astype(F32) * attn_lam_k1[idx].astype(F32)))
                   - jnp.exp(jnp.sum(attn_lam_q2[idx].astype(F32) * attn_lam_k2[idx].astype(F32)))
                   + lambda_init)
            k_past = cache_k[idx, page_table].reshape(dec_b, past, N_HEADS, 2 * HEAD_DIM)
            v_past = cache_v[idx, page_table].reshape(dec_b, past, N_HEADS, VALUE_DIM)
            w = (attn_w_in[idx], attn_subln_g[idx], attn_w_out[idx])
            op, kp_new, vp_new = diff_attn_branch(xp, None, None, lam, lambda_init, *w)
            os_, ks_new, vs_new = diff_attn_branch(xs, k_past, v_past, lam, lambda_init, *w)
            k_p.append(kp_new); v_p.append(vp_new); k_s.append(ks_new); v_s.append(vs_new)
        hp = hp + op
        hs = hs + os_
    y_prompt = rmsnorm(hp, final_norm_g)
    y_sample = rmsnorm(hs, final_norm_g)
    return (y_prompt, y_sample, jnp.stack(k_p), jnp.stack(v_p), jnp.stack(k_s), jnp.stack(v_s),
            jnp.stack(sr_p), jnp.stack(si_p), jnp.stack(sr_s), jnp.stack(si_s))
```

```python
import functools
import math

import jax
import jax.numpy as jnp
from jax import lax
from jax.experimental import pallas as pl
from jax.experimental.pallas import tpu as pltpu

F32 = jnp.float32
BF16 = jnp.bfloat16

D_MODEL = 4096
S5_GROUP = 16
S5_STATE = 64
S5_GROUPS = D_MODEL // S5_GROUP
GROUPS_PER_BLOCK = 16
N_GROUP_BLOCKS = S5_GROUPS // GROUPS_PER_BLOCK
BLOCK_CH = GROUPS_PER_BLOCK * S5_GROUP
BLOCK_ST = GROUPS_PER_BLOCK * S5_STATE
HEAD_DIM = 128
VALUE_DIM = 2 * HEAD_DIM
N_HEADS = D_MODEL // VALUE_DIM
PAGE_SIZE = 128
EPS = 1e-6
ATTN_SCALE = 1.0 / math.sqrt(HEAD_DIM)
NEG_BIG = -1e30
SUBLANES = 8
VMEM_LIMIT = 56 * 1024 * 1024


def _cparams(sem):
    return pltpu.CompilerParams(dimension_semantics=sem, vmem_limit_bytes=VMEM_LIMIT)


def _rmsnorm_kernel(x_ref, g_ref, o_ref):
    x = x_ref[...]
    ms = jnp.mean(x * x, axis=-1, keepdims=True)
    o_ref[...] = (x * lax.rsqrt(ms + EPS) * g_ref[...]).astype(o_ref.dtype)


def _rmsnorm(x, g, out_dtype):
    m, d = x.shape
    tm = min(m, 256)
    return pl.pallas_call(
        _rmsnorm_kernel,
        grid=(m // tm,),
        in_specs=[pl.BlockSpec((tm, d), lambda i: (i, 0)),
                  pl.BlockSpec((1, d), lambda i: (0, 0))],
        out_specs=pl.BlockSpec((tm, d), lambda i: (i, 0)),
        out_shape=jax.ShapeDtypeStruct((m, d), out_dtype),
        compiler_params=_cparams(("parallel",)),
        name="rmsnorm",
    )(x, g.reshape(1, d))


def _mm_kernel(epilogue, n_extra, a_ref, w_ref, *refs):
    acc = jnp.dot(a_ref[...], w_ref[...], preferred_element_type=F32)
    epilogue(acc, refs[:n_extra], refs[n_extra:])


def _matmul(a, w, col0, ncols, epilogue, extras, outs, name, tn=512):
    m, k = a.shape
    tm = min(m, 1024)
    tn = min(ncols, tn)
    assert m % tm == 0 and ncols % tn == 0 and col0 % tn == 0
    cb0 = col0 // tn
    in_specs = [pl.BlockSpec((tm, k), lambda i, j: (i, 0)),
                pl.BlockSpec((k, tn), lambda i, j: (0, cb0 + j))]
    args = [a, w]
    for arr, kind in extras:
        if kind == "mn":
            in_specs.append(pl.BlockSpec((tm, tn), lambda i, j: (i, j)))
        else:
            in_specs.append(pl.BlockSpec((1, tn), lambda i, j: (0, j)))
        args.append(arr)
    out_specs = [pl.BlockSpec((tm, tn), lambda i, j: (i, j)) for _ in outs]
    out_shape = [jax.ShapeDtypeStruct((m, ncols), dt) for dt in outs]
    res = pl.pallas_call(
        functools.partial(_mm_kernel, epilogue, len(extras)),
        grid=(m // tm, ncols // tn),
        in_specs=in_specs,
        out_specs=out_specs,
        out_shape=out_shape,
        compiler_params=_cparams(("parallel", "parallel")),
        name=name,
    )(*args)
    return res


def _epi_f32_bf16(acc, extra, outs):
    outs[0][...] = acc
    outs[1][...] = acc.astype(BF16)


def _epi_f32(acc, extra, outs):
    outs[0][...] = acc


def _epi_silu(acc, extra, outs):
    outs[0][...] = acc * jax.nn.sigmoid(acc)


def _epi_scaled_bf16(acc, extra, outs):
    outs[0][...] = (acc * ATTN_SCALE).astype(BF16)


def _epi_residual(acc, extra, outs):
    outs[0][...] = extra[0][...] + acc


def _epi_glu(acc, extra, outs):
    g = extra[0][...]
    gate = jax.nn.sigmoid(acc + extra[2][...])
    outs[0][...] = (g * gate * extra[1][...]).astype(BF16)


def _s5_disc_kernel(lr_ref, li_ref, ls_ref, br_ref, bi_ref, ar_ref, ai_ref, bbr_ref, bbi_ref):
    step = jnp.exp(ls_ref[...])
    lr = lr_ref[...]
    li = li_ref[...]
    mag = jnp.exp(lr * step)
    ar = mag * jnp.cos(li * step)
    ai = mag * jnp.sin(li * step)
    den = lr * lr + li * li
    nr = ar - 1.0
    ni = ai
    fr = (nr * lr + ni * li) / den
    fi = (ni * lr - nr * li) / den
    br = br_ref[...]
    bi = bi_ref[...]
    ar_ref[...] = ar
    ai_ref[...] = ai
    bbr_ref[...] = fr * br - fi * bi
    bbi_ref[...] = fr * bi + fi * br


def _s5_pow_kernel(lr_ref, li_ref, st_ref, pr_ref, pi_ref):
    rows = pr_ref.shape[0]
    kk = (lax.broadcasted_iota(jnp.int32, (rows, 1), 0) + 1).astype(F32)
    st = jnp.exp(st_ref[...])
    ph_r = lr_ref[...] * st
    ph_i = li_ref[...] * st
    mag = jnp.exp(ph_r * kk)
    pr_ref[...] = mag * jnp.cos(ph_i * kk)
    pi_ref[...] = mag * jnp.sin(ph_i * kk)


def _s5_prepare(lam_re, lam_im, log_step, b_re, b_im, c_re, c_im, seg_len):
    g, p, c = S5_GROUPS, S5_STATE, S5_GROUP
    cp = c * p
    lr_b = jnp.broadcast_to(lam_re[:, None, :], (g, c, p)).reshape(g, cp)
    li_b = jnp.broadcast_to(lam_im[:, None, :], (g, c, p)).reshape(g, cp)
    br_t = jnp.swapaxes(b_re, 1, 2).reshape(g, cp)
    bi_t = jnp.swapaxes(b_im, 1, 2).reshape(g, cp)
    full = pl.BlockSpec((g, cp), lambda: (0, 0))
    ar_b, ai_b, bbr_t, bbi_t = pl.pallas_call(
        _s5_disc_kernel,
        in_specs=[full, full, pl.BlockSpec((g, 1), lambda: (0, 0)), full, full],
        out_specs=[full] * 4,
        out_shape=[jax.ShapeDtypeStruct((g, cp), F32)] * 4,
        name="s5_discretize",
    )(lr_b, li_b, log_step.reshape(g, 1), br_t, bi_t)
    a_re = ar_b[:, :p].reshape(1, g * p)
    a_im = ai_b[:, :p].reshape(1, g * p)

    nb, gl = N_GROUP_BLOCKS, GROUPS_PER_BLOCK
    eye = jnp.eye(gl, dtype=F32)
    def in_proj(bb):
        return jnp.einsum('bgcp,gh->bgchp', bb.reshape(nb, gl, c, p), eye).reshape(nb, gl * c, gl * p)
    wb = jnp.concatenate([in_proj(bbr_t), in_proj(bbi_t)], axis=-1).astype(BF16)
    def out_proj(cc):
        return jnp.einsum('bgcp,gh->bgphc', cc.reshape(nb, gl, c, p), eye).reshape(nb, gl * p, gl * c)
    wc = jnp.concatenate([out_proj(c_re), out_proj(-c_im)], axis=1).astype(BF16)

    n = g * p
    row = pl.BlockSpec((1, n), lambda: (0, 0))
    tab = pl.BlockSpec((seg_len, n), lambda: (0, 0))
    st_b = jnp.broadcast_to(log_step[:, None], (g, p)).reshape(1, n)
    p_re, p_im = pl.pallas_call(
        _s5_pow_kernel,
        in_specs=[row, row, row],
        out_specs=[tab, tab],
        out_shape=[jax.ShapeDtypeStruct((seg_len, n), F32)] * 2,
        name="s5_powers",
    )(lam_re.reshape(1, n), lam_im.reshape(1, n), st_b)
    seg_mask = (jnp.arange(SUBLANES) >= SUBLANES // 2).astype(F32)
    def expand(t):
        return (t[:, None, :] * seg_mask[None, :, None]).reshape(seg_len * SUBLANES, n)
    return dict(a_re=a_re, a_im=a_im, wb=wb, wc=wc, p_re=expand(p_re), p_im=expand(p_im))


def _s5_scan_kernel(two_seg, seg_len, *refs):
    if two_seg:
        (u_ref, wb_ref, wc_ref, ar_ref, ai_ref, d_ref, pr_ref, pi_ref,
         g32_ref, g16_ref, hr_out, hi_out, up_ref, bu_ref, hb_ref, carry_ref) = refs
    else:
        (u_ref, wb_ref, wc_ref, ar_ref, ai_ref, d_ref, h0r_ref, h0i_ref,
         g32_ref, g16_ref, hr_out, hi_out, up_ref, bu_ref, hb_ref, carry_ref) = refs
    n_chain = SUBLANES
    rows = seg_len * n_chain
    ns = BLOCK_ST
    i = pl.program_id(1)
    last = pl.num_programs(1) - 1

    @pl.when(i == 0)
    def _():
        if two_seg:
            carry_ref[...] = jnp.zeros_like(carry_ref)
        else:
            carry_ref[:, :ns] = h0r_ref[...]
            carry_ref[:, ns:] = h0i_ref[...]

    for j in range(n_chain):
        if two_seg:
            b, s = j % 4, j // 4
            up_ref[:, j, :] = u_ref[b, s * seg_len:(s + 1) * seg_len, :]
        else:
            up_ref[:, j, :] = u_ref[j, :, :]
    upf = up_ref[...].reshape(rows, BLOCK_CH)
    bu_ref[...] = jnp.dot(upf.astype(BF16), wb_ref[0], preferred_element_type=F32)

    ar = jnp.broadcast_to(ar_ref[...], (n_chain, ns))
    ai = jnp.broadcast_to(ai_ref[...], (n_chain, ns))

    def step(k, carry):
        hr, hi = carry
        r = pl.multiple_of(k * n_chain, n_chain)
        nhr = ar * hr - ai * hi + bu_ref[pl.ds(r, n_chain), :ns]
        nhi = ar * hi + ai * hr + bu_ref[pl.ds(r, n_chain), ns:]
        bu_ref[pl.ds(r, n_chain), :ns] = nhr
        bu_ref[pl.ds(r, n_chain), ns:] = nhi
        return nhr, nhi

    hr, hi = lax.fori_loop(0, seg_len, step, (carry_ref[:, :ns], carry_ref[:, ns:]),
                           unroll=min(seg_len, 4))

    if two_seg:
        hm_r = pltpu.roll(hr, n_chain // 2, 0)
        hm_i = pltpu.roll(hi, n_chain // 2, 0)

        def fix(k2, _):
            r = pl.multiple_of(k2 * 2 * n_chain, 2 * n_chain)
            outs_r, outs_i = [], []
            for h in range(2):
                rr = r + h * n_chain
                pr = pr_ref[pl.ds(rr, n_chain), :]
                pi = pi_ref[pl.ds(rr, n_chain), :]
                outs_r.append(bu_ref[pl.ds(rr, n_chain), :ns] + (pr * hm_r - pi * hm_i))
                outs_i.append(bu_ref[pl.ds(rr, n_chain), ns:] + (pr * hm_i + pi * hm_r))
            hb_ref[pl.ds(r, 2 * n_chain), :ns] = jnp.concatenate(outs_r, axis=0).astype(BF16)
            hb_ref[pl.ds(r, 2 * n_chain), ns:] = jnp.concatenate(outs_i, axis=0).astype(BF16)
            return 0

        lax.fori_loop(0, seg_len // 2, fix, 0, unroll=2)
        pr = pr_ref[rows - n_chain:, :]
        pi = pi_ref[rows - n_chain:, :]
        end_r = hr + (pr * hm_r - pi * hm_i)
        end_i = hi + (pr * hm_i + pi * hm_r)
        first_half = lax.broadcasted_iota(jnp.int32, (n_chain, ns), 0) < n_chain // 2
        carry_ref[:, :ns] = jnp.where(first_half, pltpu.roll(end_r, n_chain // 2, 0), 0.0)
        carry_ref[:, ns:] = jnp.where(first_half, pltpu.roll(end_i, n_chain // 2, 0), 0.0)

        @pl.when(i == last)
        def _():
            hr_out[...] = end_r[n_chain // 2:, :]
            hi_out[...] = end_i[n_chain // 2:, :]
    else:
        hb_ref[...] = bu_ref[...].astype(BF16)
        carry_ref[:, :ns] = hr
        carry_ref[:, ns:] = hi

        @pl.when(i == last)
        def _():
            hr_out[...] = hr
            hi_out[...] = hi

    y = jnp.dot(hb_ref[...], wc_ref[0], preferred_element_type=F32)
    y = y + d_ref[...] * upf
    gl = jax.nn.gelu(y)
    up_ref[...] = gl.reshape(seg_len, n_chain, BLOCK_CH)
    for j in range(n_chain):
        v = up_ref[:, j, :]
        if two_seg:
            b, s = j % 4, j // 4
            g32_ref[b, s * seg_len:(s + 1) * seg_len, :] = v
            g16_ref[b, s * seg_len:(s + 1) * seg_len, :] = v.astype(BF16)
        else:
            g32_ref[j, :, :] = v
            g16_ref[j, :, :] = v.astype(BF16)


def _s5_scan(u, prep, d, h0r, h0i, seg_len):
    bsz, seq, _ = u.shape
    two_seg = h0r is None
    n_st = S5_GROUPS * S5_STATE
    if two_seg:
        assert bsz == SUBLANES // 2 and seq % (2 * seg_len) == 0 and seg_len % 16 == 0
        t_tile = 2 * seg_len
    else:
        assert bsz == SUBLANES and seq == seg_len
        t_tile = seg_len
    nt = seq // t_tile
    rows = seg_len * SUBLANES
    ublock = pl.BlockSpec((bsz, t_tile, BLOCK_CH), lambda gb, i: (0, i, gb))
    strow = pl.BlockSpec((1, BLOCK_ST), lambda gb, i: (0, gb))
    in_specs = [ublock,
                pl.BlockSpec((1, BLOCK_CH, 2 * BLOCK_ST), lambda gb, i: (gb, 0, 0)),
                pl.BlockSpec((1, 2 * BLOCK_ST, BLOCK_CH), lambda gb, i: (gb, 0, 0)),
                strow, strow,
                pl.BlockSpec((1, BLOCK_CH), lambda gb, i: (0, gb))]
    args = [u, prep["wb"], prep["wc"], prep["a_re"], prep["a_im"], d]
    if two_seg:
        tab = pl.BlockSpec((rows, BLOCK_ST), lambda gb, i: (0, gb))
        in_specs += [tab, tab]
        args += [prep["p_re"], prep["p_im"]]
    else:
        st = pl.BlockSpec((bsz, BLOCK_ST), lambda gb, i: (0, gb))
        in_specs += [st, st]
        args += [h0r, h0i]
    stout = pl.BlockSpec((bsz, BLOCK_ST), lambda gb, i: (0, gb))
    return pl.pallas_call(
        functools.partial(_s5_scan_kernel, two_seg, seg_len),
        grid=(N_GROUP_BLOCKS, nt),
        in_specs=in_specs,
        out_specs=[ublock, ublock, stout, stout],
        out_shape=[jax.ShapeDtypeStruct(u.shape, F32), jax.ShapeDtypeStruct(u.shape, BF16),
                   jax.ShapeDtypeStruct((bsz, n_st), F32), jax.ShapeDtypeStruct((bsz, n_st), F32)],
        scratch_shapes=[pltpu.VMEM((seg_len, SUBLANES, BLOCK_CH), F32),
                        pltpu.VMEM((rows, 2 * BLOCK_ST), F32),
                        pltpu.VMEM((rows, 2 * BLOCK_ST), BF16),
                        pltpu.VMEM((SUBLANES, 2 * BLOCK_ST), F32)],
        compiler_params=_cparams(("parallel", "arbitrary")),
        name="s5_scan_prompt" if two_seg else "s5_scan_sample",
    )(*args)


def _lambda_value(lq1, lk1, lq2, lk2, lambda_init):
    return (jnp.exp(jnp.sum(lq1 * lk1, axis=-1, keepdims=True))
            - jnp.exp(jnp.sum(lq2 * lk2, axis=-1, keepdims=True)) + lambda_init)


def _subln_gate(o, subln_g, lambda_init, sz):
    ms = jnp.mean(o * o, axis=-1, keepdims=True)
    return (o * lax.rsqrt(ms + EPS) * subln_g) * (1.0 - lambda_init) * sz


def _flash_kernel(lambda_init, tq, tk, q_ref, k_ref, v_ref, sz_ref, sg_ref,
                  lq1_ref, lk1_ref, lq2_ref, lk2_ref, o_ref, m_ref, l_ref, acc_ref):
    qi = pl.program_id(2)
    ki = pl.program_id(3)

    @pl.when(ki == 0)
    def _():
        m_ref[...] = jnp.full_like(m_ref, NEG_BIG)
        l_ref[...] = jnp.zeros_like(l_ref)
        acc_ref[...] = jnp.zeros_like(acc_ref)

    def update(masked):
        v = v_ref[...]
        if masked:
            keep = (lax.broadcasted_iota(jnp.int32, (tq, tk), 1)
                    <= lax.broadcasted_iota(jnp.int32, (tq, tk), 0))
        for c in range(2):
            q = q_ref[:, c * HEAD_DIM:(c + 1) * HEAD_DIM]
            k = k_ref[:, c * HEAD_DIM:(c + 1) * HEAD_DIM]
            s = lax.dot_general(q, k, (((1,), (1,)), ((), ())), preferred_element_type=F32)
            if masked:
                s = jnp.where(keep, s, NEG_BIG)
            m_old = m_ref[c]
            m_new = jnp.maximum(m_old, jnp.max(s, axis=-1, keepdims=True))
            alpha = jnp.exp(m_old - m_new)
            p = jnp.exp(s - m_new)
            l_ref[c] = alpha * l_ref[c] + jnp.sum(p, axis=-1, keepdims=True)
            acc_ref[c] = alpha * acc_ref[c] + jnp.dot(p.astype(BF16), v, preferred_element_type=F32)
            m_ref[c] = m_new

    @pl.when(ki < qi)
    def _():
        update(False)

    @pl.when(ki == qi)
    def _():
        update(True)
        lam = _lambda_value(lq1_ref[...], lk1_ref[...], lq2_ref[...], lk2_ref[...], lambda_init)
        o = acc_ref[0] / l_ref[0] - lam * (acc_ref[1] / l_ref[1])
        o_ref[...] = _subln_gate(o, sg_ref[...], lambda_init, sz_ref[...]).astype(o_ref.dtype)


def _flash_attention(q, k, v, sz, subln_g, lam_vecs, lambda_init, bsz, seq):
    tq = tk = min(seq, 512)
    nq = seq // tq
    qspec = pl.BlockSpec((tq, VALUE_DIM), lambda b, h, qi, ki: (b * nq + qi, h))
    kspec = pl.BlockSpec((tk, VALUE_DIM), lambda b, h, qi, ki: (b * nq + jnp.minimum(ki, qi), h))
    vec = pl.BlockSpec((1, HEAD_DIM), lambda b, h, qi, ki: (0, 0))
    return pl.pallas_call(
        functools.partial(_flash_kernel, lambda_init, tq, tk),
        grid=(bsz, N_HEADS, nq, nq),
        in_specs=[qspec, kspec, kspec, qspec,
                  pl.BlockSpec((1, VALUE_DIM), lambda b, h, qi, ki: (0, 0)),
                  vec, vec, vec, vec],
        out_specs=qspec,
        out_shape=jax.ShapeDtypeStruct((bsz * seq, D_MODEL), BF16),
        scratch_shapes=[pltpu.VMEM((2, tq, 1), F32), pltpu.VMEM((2, tq, 1), F32),
                        pltpu.VMEM((2, tq, VALUE_DIM), F32)],
        compiler_params=_cparams(("parallel", "parallel", "parallel", "arbitrary")),
        name="flash_diff_attention",
    )(q, k, v, sz, subln_g.reshape(1, VALUE_DIM), *[x.reshape(1, HEAD_DIM) for x in lam_vecs])


def _decode_kernel(lambda_init, pages_per_step, n_tok, pt_ref, qm_ref, *refs):
    kc = refs[:pages_per_step]
    vc = refs[pages_per_step:2 * pages_per_step]
    (kn_ref, vn_ref, sz_ref, sg_ref, lq1_ref, lk1_ref, lq2_ref, lk2_ref,
     o_ref, m_ref, l_ref, acc_ref) = refs[2 * pages_per_step:]
    step = pl.program_id(1)
    last = pl.num_programs(1) - 1
    rows_per_head = 2 * n_tok

    @pl.when(step == 0)
    def _():
        m_ref[...] = jnp.full_like(m_ref, NEG_BIG)
        l_ref[...] = jnp.zeros_like(l_ref)
        acc_ref[...] = jnp.zeros_like(acc_ref)

    def update(kblk, vblk, keep):
        s = lax.dot_general(qm_ref[0], kblk, (((1,), (1,)), ((), ())), preferred_element_type=F32)
        if keep is not None:
            s = jnp.where(keep, s, NEG_BIG)
        m_old = m_ref[...]
        m_new = jnp.maximum(m_old, jnp.max(s, axis=-1, keepdims=True))
        alpha = jnp.exp(m_old - m_new)
        p = jnp.exp(s - m_new)
        l_ref[...] = alpha * l_ref[...] + jnp.sum(p, axis=-1, keepdims=True)
        m_ref[...] = m_new
        pb = p.astype(BF16)
        for h in range(N_HEADS):
            r0 = h * rows_per_head
            pv = jnp.dot(pb[r0:r0 + rows_per_head, :], vblk[:, h * VALUE_DIM:(h + 1) * VALUE_DIM],
                         preferred_element_type=F32)
            acc_ref[r0:r0 + rows_per_head, :] = alpha[r0:r0 + rows_per_head] * acc_ref[r0:r0 + rows_per_head, :] + pv

    kblk = jnp.concatenate([r[0].astype(BF16) for r in kc], axis=0)
    vblk = jnp.concatenate([r[0].astype(BF16) for r in vc], axis=0)
    update(kblk, vblk, None)

    @pl.when(step == last)
    def _():
        n_rows = N_HEADS * rows_per_head
        tok = lax.broadcasted_iota(jnp.int32, (n_rows, PAGE_SIZE), 0) % n_tok
        pos = lax.broadcasted_iota(jnp.int32, (n_rows, PAGE_SIZE), 1)
        update(kn_ref[0], vn_ref[0], pos <= tok)
        lam = _lambda_value(lq1_ref[...], lk1_ref[...], lq2_ref[...], lk2_ref[...], lambda_init)
        on = acc_ref[...] / l_ref[...]
        for h in range(N_HEADS):
            r0 = h * rows_per_head
            o = on[r0:r0 + n_tok, :] - lam * on[r0 + n_tok:r0 + 2 * n_tok, :]
            cols = slice(h * VALUE_DIM, (h + 1) * VALUE_DIM)
            o_ref[0, :, cols] = _subln_gate(o, sg_ref[...], lambda_init, sz_ref[0, :, cols]).astype(o_ref.dtype)


def _decode_attention(q, k_new, v_new, sz, cache_k, cache_v, page_off, page_table, subln_g, lam_vecs,
                      lambda_init, bsz, n_tok):
    n_pages = page_table.shape[1]
    pps = 4
    assert n_pages % pps == 0
    rows = N_HEADS * 2 * n_tok
    q5 = q.reshape(bsz, n_tok, N_HEADS, 2, HEAD_DIM)
    eye = jnp.eye(N_HEADS * 2, dtype=BF16).reshape(N_HEADS, 2, N_HEADS, 2)
    qm = jnp.einsum('bthcd,hcgk->bhctgkd', q5, eye).reshape(bsz, rows, D_MODEL)

    def pad_page(x):
        x = x.reshape(bsz, n_tok, D_MODEL)
        return jnp.pad(x, ((0, 0), (0, PAGE_SIZE - n_tok), (0, 0)))

    def page_spec(r):
        return pl.BlockSpec((1, PAGE_SIZE, D_MODEL),
                            lambda b, s, pt: (page_off + pt[b, s * pps + r], 0, 0))
    newspec = pl.BlockSpec((1, PAGE_SIZE, D_MODEL), lambda b, s, pt: (b, 0, 0))
    tokspec = pl.BlockSpec((1, n_tok, D_MODEL), lambda b, s, pt: (b, 0, 0))
    vec = pl.BlockSpec((1, HEAD_DIM), lambda b, s, pt: (0, 0))
    grid_spec = pltpu.PrefetchScalarGridSpec(
        num_scalar_prefetch=1,
        grid=(bsz, n_pages // pps),
        in_specs=[pl.BlockSpec((1, rows, D_MODEL), lambda b, s, pt: (b, 0, 0))]
                 + [page_spec(r) for r in range(pps)] + [page_spec(r) for r in range(pps)]
                 + [newspec, newspec, tokspec,
                    pl.BlockSpec((1, VALUE_DIM), lambda b, s, pt: (0, 0)), vec, vec, vec, vec],
        out_specs=tokspec,
        scratch_shapes=[pltpu.VMEM((rows, 1), F32), pltpu.VMEM((rows, 1), F32),
                        pltpu.VMEM((rows, VALUE_DIM), F32)],
    )
    out = pl.pallas_call(
        functools.partial(_decode_kernel, lambda_init, pps, n_tok),
        grid_spec=grid_spec,
        out_shape=jax.ShapeDtypeStruct((bsz, n_tok, D_MODEL), BF16),
        compiler_params=_cparams(("parallel", "arbitrary")),
        name="decode_diff_attention",
    )(page_table, qm, *([cache_k] * pps), *([cache_v] * pps),
      pad_page(k_new), pad_page(v_new),
      sz.reshape(bsz, n_tok, D_MODEL), subln_g.reshape(1, VALUE_DIM),
      *[x.reshape(1, HEAD_DIM) for x in lam_vecs])
    return out.reshape(bsz * n_tok, D_MODEL)


def _lambda_init(layer):
    return 0.8 - 0.6 * math.exp(-0.3 * layer)


def _s5_layer(h, xn, w_in, w_glu, b_glu, w_out, prep, d, h0r, h0i, bsz, seq, seg_len):
    u, = _matmul(xn, w_in, 0, D_MODEL, _epi_f32, [], [F32], "s5_in_u")
    sz, = _matmul(xn, w_in, D_MODEL, D_MODEL, _epi_silu, [], [F32], "s5_in_z")
    g32, g16, hr, hi = _s5_scan(u.reshape(bsz, seq, D_MODEL), prep, d, h0r, h0i, seg_len)
    g32 = g32.reshape(bsz * seq, D_MODEL)
    g16 = g16.reshape(bsz * seq, D_MODEL)
    m, = _matmul(g16, w_glu, 0, D_MODEL, _epi_glu,
                 [(g32, "mn"), (sz, "mn"), (b_glu.reshape(1, D_MODEL), "n")], [BF16], "s5_glu", tn=512)
    h_new, = _matmul(m, w_out, 0, D_MODEL, _epi_residual, [(h, "mn")], [F32], "s5_out")
    shape = (bsz, S5_GROUPS, S5_STATE)
    return h_new, hr.reshape(shape), hi.reshape(shape)


def _attn_in_proj(xn, w_in):
    q, = _matmul(xn, w_in, 0, D_MODEL, _epi_scaled_bf16, [], [BF16], "attn_in_q")
    k32, k16 = _matmul(xn, w_in, D_MODEL, D_MODEL, _epi_f32_bf16, [], [F32, BF16], "attn_in_k")
    v32, v16 = _matmul(xn, w_in, 2 * D_MODEL, D_MODEL, _epi_f32_bf16, [], [F32, BF16], "attn_in_v")
    sz, = _matmul(xn, w_in, 3 * D_MODEL, D_MODEL, _epi_silu, [], [F32], "attn_in_z")
    return q, k32, k16, v32, v16, sz


def kernel(x_prompt, x_sample, cache_k, cache_v, state_s5_re, state_s5_im, page_table, norm_g, final_norm_g, s5_w_in, s5_lambda_re, s5_lambda_im, s5_log_step, s5_b_re, s5_b_im, s5_c_re, s5_c_im, s5_d, s5_w_glu, s5_b_glu, s5_w_out, attn_w_in, attn_lam_q1, attn_lam_k1, attn_lam_q2, attn_lam_k2, attn_subln_g, attn_w_out):
    bp, lp, _ = x_prompt.shape
    bs, ls, _ = x_sample.shape
    depth = norm_g.shape[0]
    n_pool = cache_k.shape[1]
    seg_len = min(128, lp // 2)

    hp = x_prompt.reshape(bp * lp, D_MODEL)
    hs = x_sample.reshape(bs * ls, D_MODEL)
    k_p, v_p, k_s, v_s = [], [], [], []
    sr_p, si_p, sr_s, si_s = [], [], [], []
    for layer in range(depth):
        idx = layer // 2
        xp = _rmsnorm(hp, norm_g[layer], BF16)
        xs = _rmsnorm(hs, norm_g[layer], BF16)
        if layer % 2 == 0:
            prep = _s5_prepare(s5_lambda_re[idx], s5_lambda_im[idx], s5_log_step[idx],
                               s5_b_re[idx], s5_b_im[idx], s5_c_re[idx], s5_c_im[idx], seg_len)
            w_in = s5_w_in[idx].astype(BF16)
            w_glu = s5_w_glu[idx].astype(BF16)
            w_out = s5_w_out[idx].astype(BF16)
            d = s5_d[idx].reshape(1, D_MODEL)
            hp, hr, hi = _s5_layer(hp, xp, w_in, w_glu, s5_b_glu[idx], w_out, prep, d,
                                   None, None, bp, lp, seg_len)
            sr_p.append(hr); si_p.append(hi)
            h0r = state_s5_re[idx].reshape(bs, S5_GROUPS * S5_STATE)
            h0i = state_s5_im[idx].reshape(bs, S5_GROUPS * S5_STATE)
            hs, hr, hi = _s5_layer(hs, xs, w_in, w_glu, s5_b_glu[idx], w_out, prep, d,
                                   h0r, h0i, bs, ls, ls)
            sr_s.append(hr); si_s.append(hi)
        else:
            lam0 = _lambda_init(layer)
            lam_vecs = (attn_lam_q1[idx], attn_lam_k1[idx], attn_lam_q2[idx], attn_lam_k2[idx])
            w_in = attn_w_in[idx].astype(BF16)
            w_out = attn_w_out[idx].astype(BF16)
            q, k32, k16, v32, v16, sz = _attn_in_proj(xp, w_in)
            m = _flash_attention(q, k16, v16, sz, attn_subln_g[idx], lam_vecs, lam0, bp, lp)
            hp, = _matmul(m, w_out, 0, D_MODEL, _epi_residual, [(hp, "mn")], [F32], "attn_out")
            k_p.append(k32.reshape(bp, lp, N_HEADS, VALUE_DIM))
            v_p.append(v32.reshape(bp, lp, N_HEADS, VALUE_DIM))
            q, k32, k16, v32, v16, sz = _attn_in_proj(xs, w_in)
            m = _decode_attention(q, k16, v16, sz,
                                  cache_k.reshape(-1, PAGE_SIZE, D_MODEL),
                                  cache_v.reshape(-1, PAGE_SIZE, D_MODEL),
                                  idx * n_pool, page_table, attn_subln_g[idx], lam_vecs, lam0, bs, ls)
            hs, = _matmul(m, w_out, 0, D_MODEL, _epi_residual, [(hs, "mn")], [F32], "attn_out")
            k_s.append(k32.reshape(bs, ls, N_HEADS, VALUE_DIM))
            v_s.append(v32.reshape(bs, ls, N_HEADS, VALUE_DIM))
    y_prompt = _rmsnorm(hp, final_norm_g, F32).reshape(bp, lp, D_MODEL)
    y_sample = _rmsnorm(hs, final_norm_g, F32).reshape(bs, ls, D_MODEL)
    return (y_prompt, y_sample, jnp.stack(k_p), jnp.stack(v_p), jnp.stack(k_s), jnp.stack(v_s),
            jnp.stack(sr_p), jnp.stack(si_p), jnp.stack(sr_s), jnp.stack(si_s))
```

```python
import functools
import math

import jax
import jax.numpy as jnp
from jax import lax
from jax.experimental import pallas as pl
from jax.experimental.pallas import tpu as pltpu

F32 = jnp.float32
BF16 = jnp.bfloat16

D_MODEL = 4096
S5_GROUP = 16
S5_STATE = 64
S5_GROUPS = D_MODEL // S5_GROUP
GROUPS_PER_BLOCK = 16
N_GROUP_BLOCKS = S5_GROUPS // GROUPS_PER_BLOCK
BLOCK_CH = GROUPS_PER_BLOCK * S5_GROUP
BLOCK_ST = GROUPS_PER_BLOCK * S5_STATE
HEAD_DIM = 128
VALUE_DIM = 2 * HEAD_DIM
N_HEADS = D_MODEL // VALUE_DIM
PAGE_SIZE = 128
EPS = 1e-6
ATTN_SCALE = 1.0 / math.sqrt(HEAD_DIM)
Q_SCALE = ATTN_SCALE * math.log2(math.e)
NEG_BIG = -1e30
SUBLANES = 8
LANES = 128
FLASH_TQ = 512
FLASH_TK = 512
VMEM_LIMIT = 56 * 1024 * 1024


def _cparams(sem):
    return pltpu.CompilerParams(dimension_semantics=sem, vmem_limit_bytes=VMEM_LIMIT)


def _rmsnorm_kernel(x_ref, g_ref, o_ref):
    x = x_ref[...]
    ms = jnp.mean(x * x, axis=-1, keepdims=True)
    o_ref[...] = (x * lax.rsqrt(ms + EPS) * g_ref[...]).astype(o_ref.dtype)


def _rmsnorm(x, g, out_dtype):
    m, d = x.shape
    tm = min(m, 256)
    return pl.pallas_call(
        _rmsnorm_kernel,
        grid=(m // tm,),
        in_specs=[pl.BlockSpec((tm, d), lambda i: (i, 0)),
                  pl.BlockSpec((1, d), lambda i: (0, 0))],
        out_specs=pl.BlockSpec((tm, d), lambda i: (i, 0)),
        out_shape=jax.ShapeDtypeStruct((m, d), out_dtype),
        compiler_params=_cparams(("parallel",)),
        name="rmsnorm",
    )(x, g.reshape(1, d))


def _mm_kernel(epilogue, n_extra, a_ref, w_ref, *refs):
    acc = jnp.dot(a_ref[...], w_ref[...], preferred_element_type=F32)
    epilogue(acc, refs[:n_extra], refs[n_extra:])


def _matmul(a, w, col0, ncols, epilogue, extras, outs, name, tn=512):
    m, k = a.shape
    tm = min(m, 1024)
    tn = min(ncols, tn)
    assert m % tm == 0 and ncols % tn == 0 and col0 % tn == 0
    cb0 = col0 // tn
    in_specs = [pl.BlockSpec((tm, k), lambda i, j: (i, 0)),
                pl.BlockSpec((k, tn), lambda i, j: (0, cb0 + j))]
    args = [a, w]
    for arr, kind in extras:
        if kind == "mn":
            in_specs.append(pl.BlockSpec((tm, tn), lambda i, j: (i, j)))
        else:
            in_specs.append(pl.BlockSpec((1, tn), lambda i, j: (0, j)))
        args.append(arr)
    out_specs = [pl.BlockSpec((tm, tn), lambda i, j: (i, j)) for _ in outs]
    out_shape = [jax.ShapeDtypeStruct((m, ncols), dt) for dt in outs]
    res = pl.pallas_call(
        functools.partial(_mm_kernel, epilogue, len(extras)),
        grid=(m // tm, ncols // tn),
        in_specs=in_specs,
        out_specs=out_specs,
        out_shape=out_shape,
        compiler_params=_cparams(("parallel", "parallel")),
        name=name,
    )(*args)
    return res


def _epi_f32_bf16(acc, extra, outs):
    outs[0][...] = acc
    outs[1][...] = acc.astype(BF16)


def _epi_f32(acc, extra, outs):
    outs[0][...] = acc


def _epi_silu(acc, extra, outs):
    outs[0][...] = acc * jax.nn.sigmoid(acc)


def _epi_scaled_bf16(acc, extra, outs):
    outs[0][...] = (acc * Q_SCALE).astype(BF16)


def _epi_residual(acc, extra, outs):
    outs[0][...] = extra[0][...] + acc


def _epi_glu(acc, extra, outs):
    g = extra[0][...]
    gate = jax.nn.sigmoid(acc + extra[2][...])
    outs[0][...] = (g * gate * extra[1][...]).astype(BF16)


def _s5_disc_kernel(lr_ref, li_ref, ls_ref, br_ref, bi_ref, ar_ref, ai_ref, bbr_ref, bbi_ref):
    step = jnp.exp(ls_ref[...])
    lr = lr_ref[...]
    li = li_ref[...]
    mag = jnp.exp(lr * step)
    ar = mag * jnp.cos(li * step)
    ai = mag * jnp.sin(li * step)
    den = lr * lr + li * li
    nr = ar - 1.0
    ni = ai
    fr = (nr * lr + ni * li) / den
    fi = (ni * lr - nr * li) / den
    br = br_ref[...]
    bi = bi_ref[...]
    ar_ref[...] = ar
    ai_ref[...] = ai
    bbr_ref[...] = fr * br - fi * bi
    bbi_ref[...] = fr * bi + fi * br


def _s5_pow_kernel(lr_ref, li_ref, st_ref, pr_ref, pi_ref):
    rows = pr_ref.shape[0]
    kk = (lax.broadcasted_iota(jnp.int32, (rows, 1), 0) + 1).astype(F32)
    st = jnp.exp(st_ref[...])
    ph_r = lr_ref[...] * st
    ph_i = li_ref[...] * st
    mag = jnp.exp(ph_r * kk)
    pr_ref[...] = mag * jnp.cos(ph_i * kk)
    pi_ref[...] = mag * jnp.sin(ph_i * kk)


def _s5_prepare(lam_re, lam_im, log_step, b_re, b_im, c_re, c_im, seg_len):
    g, p, c = S5_GROUPS, S5_STATE, S5_GROUP
    cp = c * p
    lr_b = jnp.broadcast_to(lam_re[:, None, :], (g, c, p)).reshape(g, cp)
    li_b = jnp.broadcast_to(lam_im[:, None, :], (g, c, p)).reshape(g, cp)
    br_t = jnp.swapaxes(b_re, 1, 2).reshape(g, cp)
    bi_t = jnp.swapaxes(b_im, 1, 2).reshape(g, cp)
    full = pl.BlockSpec((g, cp), lambda: (0, 0))
    ar_b, ai_b, bbr_t, bbi_t = pl.pallas_call(
        _s5_disc_kernel,
        in_specs=[full, full, pl.BlockSpec((g, 1), lambda: (0, 0)), full, full],
        out_specs=[full] * 4,
        out_shape=[jax.ShapeDtypeStruct((g, cp), F32)] * 4,
        name="s5_discretize",
    )(lr_b, li_b, log_step.reshape(g, 1), br_t, bi_t)
    a_re = ar_b[:, :p].reshape(1, g * p)
    a_im = ai_b[:, :p].reshape(1, g * p)

    nb, gl = N_GROUP_BLOCKS, GROUPS_PER_BLOCK
    eye = jnp.eye(gl, dtype=F32)
    def in_proj(bb):
        return jnp.einsum('bgcp,gh->bgchp', bb.reshape(nb, gl, c, p), eye).reshape(nb, gl * c, gl * p)
    wb = jnp.concatenate([in_proj(bbr_t), in_proj(bbi_t)], axis=-1).astype(BF16)
    def out_proj(cc):
        return jnp.einsum('bgcp,gh->bgphc', cc.reshape(nb, gl, c, p), eye).reshape(nb, gl * p, gl * c)
    wc = jnp.concatenate([out_proj(c_re), out_proj(-c_im)], axis=1).astype(BF16)

    n = g * p
    row = pl.BlockSpec((1, n), lambda: (0, 0))
    tab = pl.BlockSpec((seg_len, n), lambda: (0, 0))
    st_b = jnp.broadcast_to(log_step[:, None], (g, p)).reshape(1, n)
    p_re, p_im = pl.pallas_call(
        _s5_pow_kernel,
        in_specs=[row, row, row],
        out_specs=[tab, tab],
        out_shape=[jax.ShapeDtypeStruct((seg_len, n), F32)] * 2,
        name="s5_powers",
    )(lam_re.reshape(1, n), lam_im.reshape(1, n), st_b)
    seg_mask = (jnp.arange(SUBLANES) >= SUBLANES // 2).astype(F32)
    def expand(t):
        return (t[:, None, :] * seg_mask[None, :, None]).reshape(seg_len * SUBLANES, n)
    return dict(a_re=a_re, a_im=a_im, wb=wb, wc=wc, p_re=expand(p_re), p_im=expand(p_im))


def _s5_scan_kernel(two_seg, seg_len, *refs):
    if two_seg:
        (u_ref, wb_ref, wc_ref, ar_ref, ai_ref, d_ref, pr_ref, pi_ref,
         g32_ref, g16_ref, hr_out, hi_out, up_ref, bu_ref, hb_ref, carry_ref) = refs
    else:
        (u_ref, wb_ref, wc_ref, ar_ref, ai_ref, d_ref, h0r_ref, h0i_ref,
         g32_ref, g16_ref, hr_out, hi_out, up_ref, bu_ref, hb_ref, carry_ref) = refs
    n_chain = SUBLANES
    rows = seg_len * n_chain
    ns = BLOCK_ST
    i = pl.program_id(1)
    last = pl.num_programs(1) - 1

    @pl.when(i == 0)
    def _():
        if two_seg:
            carry_ref[...] = jnp.zeros_like(carry_ref)
        else:
            carry_ref[:, :ns] = h0r_ref[...]
            carry_ref[:, ns:] = h0i_ref[...]

    for j in range(n_chain):
        if two_seg:
            b, s = j % 4, j // 4
            up_ref[:, j, :] = u_ref[b, s * seg_len:(s + 1) * seg_len, :]
        else:
            up_ref[:, j, :] = u_ref[j, :, :]
    upf = up_ref[...].reshape(rows, BLOCK_CH)
    bu_ref[...] = jnp.dot(upf.astype(BF16), wb_ref[0], preferred_element_type=F32)

    ar = jnp.broadcast_to(ar_ref[...], (n_chain, ns))
    ai = jnp.broadcast_to(ai_ref[...], (n_chain, ns))

    def step(k, carry):
        hr, hi = carry
        r = pl.multiple_of(k * n_chain, n_chain)
        nhr = ar * hr - ai * hi + bu_ref[pl.ds(r, n_chain), :ns]
        nhi = ar * hi + ai * hr + bu_ref[pl.ds(r, n_chain), ns:]
        bu_ref[pl.ds(r, n_chain), :ns] = nhr
        bu_ref[pl.ds(r, n_chain), ns:] = nhi
        return nhr, nhi

    hr, hi = lax.fori_loop(0, seg_len, step, (carry_ref[:, :ns], carry_ref[:, ns:]),
                           unroll=min(seg_len, 4))

    if two_seg:
        hm_r = pltpu.roll(hr, n_chain // 2, 0)
        hm_i = pltpu.roll(hi, n_chain // 2, 0)

        def fix(k2, _):
            r = pl.multiple_of(k2 * 2 * n_chain, 2 * n_chain)
            outs_r, outs_i = [], []
            for h in range(2):
                rr = r + h * n_chain
                pr = pr_ref[pl.ds(rr, n_chain), :]
                pi = pi_ref[pl.ds(rr, n_chain), :]
                outs_r.append(bu_ref[pl.ds(rr, n_chain), :ns] + (pr * hm_r - pi * hm_i))
                outs_i.append(bu_ref[pl.ds(rr, n_chain), ns:] + (pr * hm_i + pi * hm_r))
            hb_ref[pl.ds(r, 2 * n_chain), :ns] = jnp.concatenate(outs_r, axis=0).astype(BF16)
            hb_ref[pl.ds(r, 2 * n_chain), ns:] = jnp.concatenate(outs_i, axis=0).astype(BF16)
            return 0

        lax.fori_loop(0, seg_len // 2, fix, 0, unroll=2)
        pr = pr_ref[rows - n_chain:, :]
        pi = pi_ref[rows - n_chain:, :]
        end_r = hr + (pr * hm_r - pi * hm_i)
        end_i = hi + (pr * hm_i + pi * hm_r)
        first_half = lax.broadcasted_iota(jnp.int32, (n_chain, ns), 0) < n_chain // 2
        carry_ref[:, :ns] = jnp.where(first_half, pltpu.roll(end_r, n_chain // 2, 0), 0.0)
        carry_ref[:, ns:] = jnp.where(first_half, pltpu.roll(end_i, n_chain // 2, 0), 0.0)

        @pl.when(i == last)
        def _():
            hr_out[...] = end_r[n_chain // 2:, :]
            hi_out[...] = end_i[n_chain // 2:, :]
    else:
        hb_ref[...] = bu_ref[...].astype(BF16)
        carry_ref[:, :ns] = hr
        carry_ref[:, ns:] = hi

        @pl.when(i == last)
        def _():
            hr_out[...] = hr
            hi_out[...] = hi

    y = jnp.dot(hb_ref[...], wc_ref[0], preferred_element_type=F32)
    y = y + d_ref[...] * upf
    gl = jax.nn.gelu(y)
    up_ref[...] = gl.reshape(seg_len, n_chain, BLOCK_CH)
    for j in range(n_chain):
        v = up_ref[:, j, :]
        if two_seg:
            b, s = j % 4, j // 4
            g32_ref[b, s * seg_len:(s + 1) * seg_len, :] = v
        else:
            g32_ref[j, :, :] = v
    g16_ref[...] = g32_ref[...].astype(BF16)


def _s5_scan(u, prep, d, h0r, h0i, seg_len):
    bsz, seq, _ = u.shape
    two_seg = h0r is None
    n_st = S5_GROUPS * S5_STATE
    if two_seg:
        assert bsz == SUBLANES // 2 and seq % (2 * seg_len) == 0 and seg_len % 16 == 0
        t_tile = 2 * seg_len
    else:
        assert bsz == SUBLANES and seq == seg_len
        t_tile = seg_len
    nt = seq // t_tile
    rows = seg_len * SUBLANES
    ublock = pl.BlockSpec((bsz, t_tile, BLOCK_CH), lambda gb, i: (0, i, gb))
    strow = pl.BlockSpec((1, BLOCK_ST), lambda gb, i: (0, gb))
    in_specs = [ublock,
                pl.BlockSpec((1, BLOCK_CH, 2 * BLOCK_ST), lambda gb, i: (gb, 0, 0)),
                pl.BlockSpec((1, 2 * BLOCK_ST, BLOCK_CH), lambda gb, i: (gb, 0, 0)),
                strow, strow,
                pl.BlockSpec((1, BLOCK_CH), lambda gb, i: (0, gb))]
    args = [u, prep["wb"], prep["wc"], prep["a_re"], prep["a_im"], d]
    if two_seg:
        tab = pl.BlockSpec((rows, BLOCK_ST), lambda gb, i: (0, gb))
        in_specs += [tab, tab]
        args += [prep["p_re"], prep["p_im"]]
    else:
        st = pl.BlockSpec((bsz, BLOCK_ST), lambda gb, i: (0, gb))
        in_specs += [st, st]
        args += [h0r, h0i]
    stout = pl.BlockSpec((bsz, BLOCK_ST), lambda gb, i: (0, gb))
    return pl.pallas_call(
        functools.partial(_s5_scan_kernel, two_seg, seg_len),
        grid=(N_GROUP_BLOCKS, nt),
        in_specs=in_specs,
        out_specs=[ublock, ublock, stout, stout],
        out_shape=[jax.ShapeDtypeStruct(u.shape, F32), jax.ShapeDtypeStruct(u.shape, BF16),
                   jax.ShapeDtypeStruct((bsz, n_st), F32), jax.ShapeDtypeStruct((bsz, n_st), F32)],
        scratch_shapes=[pltpu.VMEM((seg_len, SUBLANES, BLOCK_CH), F32),
                        pltpu.VMEM((rows, 2 * BLOCK_ST), F32),
                        pltpu.VMEM((rows, 2 * BLOCK_ST), BF16),
                        pltpu.VMEM((SUBLANES, 2 * BLOCK_ST), F32)],
        compiler_params=_cparams(("parallel", "arbitrary")),
        name="s5_scan_prompt" if two_seg else "s5_scan_sample",
    )(*args)


def _lambda_value(lq1, lk1, lq2, lk2, lambda_init):
    return (jnp.exp(jnp.sum(lq1 * lk1, axis=-1, keepdims=True))
            - jnp.exp(jnp.sum(lq2 * lk2, axis=-1, keepdims=True)) + lambda_init)


def _subln_gate(o, subln_g, lambda_init, sz):
    ms = jnp.mean(o * o, axis=-1, keepdims=True)
    return (o * lax.rsqrt(ms + EPS) * subln_g) * (1.0 - lambda_init) * sz


def _flash_kernel(lambda_init, tq, tk, q_ref, k_ref, v_ref, sz_ref, sg_ref,
                  lq1_ref, lk1_ref, lq2_ref, lk2_ref, o_ref, m_ref, l_ref, acc_ref):
    qi = pl.program_id(2)
    ki = pl.program_id(3)

    @pl.when(ki == 0)
    def _():
        m_ref[...] = jnp.full_like(m_ref, NEG_BIG)
        l_ref[...] = jnp.zeros_like(l_ref)
        acc_ref[...] = jnp.zeros_like(acc_ref)

    ratio = tq // tk
    first_diag = qi * ratio
    last_needed = first_diag + ratio - 1

    def update(masked):
        v = v_ref[...]
        if masked:
            keep = (lax.broadcasted_iota(jnp.int32, (tq, tk), 1) + ki * tk
                    <= lax.broadcasted_iota(jnp.int32, (tq, tk), 0) + qi * tq)
        for c in range(2):
            q = q_ref[:, c * HEAD_DIM:(c + 1) * HEAD_DIM]
            k = k_ref[:, c * HEAD_DIM:(c + 1) * HEAD_DIM]
            s = lax.dot_general(q, k, (((1,), (1,)), ((), ())), preferred_element_type=F32)
            if masked:
                s = jnp.where(keep, s, NEG_BIG)
            m_old = m_ref[c]
            m_new = jnp.maximum(m_old, jnp.max(s, axis=-1, keepdims=True))
            alpha = jnp.exp2(m_old - m_new)
            p = jnp.exp2(s - jnp.tile(m_new, (1, tk // LANES)))
            l_ref[c] = alpha * l_ref[c] + jnp.sum(p, axis=-1, keepdims=True)
            acc_ref[c] = (jnp.tile(alpha, (1, VALUE_DIM // LANES)) * acc_ref[c]
                          + jnp.dot(p.astype(BF16), v, preferred_element_type=F32))
            m_ref[c] = m_new

    @pl.when(ki < first_diag)
    def _():
        update(False)

    @pl.when(jnp.logical_and(ki >= first_diag, ki <= last_needed))
    def _():
        update(True)

    @pl.when(ki == last_needed)
    def _():
        lam = _lambda_value(lq1_ref[...], lk1_ref[...], lq2_ref[...], lk2_ref[...], lambda_init)
        rep = VALUE_DIM // LANES
        o = (acc_ref[0] / jnp.tile(l_ref[0], (1, rep))
             - lam * (acc_ref[1] / jnp.tile(l_ref[1], (1, rep))))
        o_ref[...] = _subln_gate(o, sg_ref[...], lambda_init, sz_ref[...]).astype(o_ref.dtype)


def _flash_attention(q, k, v, sz, subln_g, lam_vecs, lambda_init, bsz, seq):
    tq = min(seq, FLASH_TQ)
    tk = min(seq, FLASH_TK)
    assert seq % tq == 0 and tq % tk == 0
    nq = seq // tq
    nk = seq // tk
    ratio = tq // tk
    qspec = pl.BlockSpec((tq, VALUE_DIM), lambda b, h, qi, ki: (b * nq + qi, h))
    kspec = pl.BlockSpec((tk, VALUE_DIM),
                         lambda b, h, qi, ki: (b * nk + jnp.minimum(ki, (qi + 1) * ratio - 1), h))
    vec = pl.BlockSpec((1, HEAD_DIM), lambda b, h, qi, ki: (0, 0))
    return pl.pallas_call(
        functools.partial(_flash_kernel, lambda_init, tq, tk),
        grid=(bsz, N_HEADS, nq, nk),
        in_specs=[qspec, kspec, kspec, qspec,
                  pl.BlockSpec((1, VALUE_DIM), lambda b, h, qi, ki: (0, 0)),
                  vec, vec, vec, vec],
        out_specs=qspec,
        out_shape=jax.ShapeDtypeStruct((bsz * seq, D_MODEL), BF16),
        scratch_shapes=[pltpu.VMEM((2, tq, LANES), F32), pltpu.VMEM((2, tq, LANES), F32),
                        pltpu.VMEM((2, tq, VALUE_DIM), F32)],
        compiler_params=_cparams(("parallel", "parallel", "parallel", "arbitrary")),
        name="flash_diff_attention",
    )(q, k, v, sz, subln_g.reshape(1, VALUE_DIM), *[x.reshape(1, HEAD_DIM) for x in lam_vecs])


HEADS_PER_STEP = SUBLANES
PAGES_PER_STEP = 4


def _decode_kernel(lambda_init, n_tok, pt_ref, qm_ref, *refs):
    pps = PAGES_PER_STEP
    kc = refs[:pps]
    vc = refs[pps:2 * pps]
    (kn_ref, vn_ref, sz_ref, sg_ref, lq1_ref, lk1_ref, lq2_ref, lk2_ref,
     o_ref, m_ref, l_ref, acc_ref, bias_ref) = refs[2 * pps:]
    step = pl.program_id(2)
    last = pl.num_programs(2) - 1
    rows_per_head = 2 * n_tok
    n_rows = HEADS_PER_STEP * rows_per_head
    page_rows = PAGE_SIZE * HEADS_PER_STEP

    @pl.when(step == 0)
    def _():
        m_ref[...] = jnp.full_like(m_ref, NEG_BIG)
        l_ref[...] = jnp.zeros_like(l_ref)
        acc_ref[...] = jnp.zeros_like(acc_ref)
        row_head = lax.broadcasted_iota(jnp.int32, bias_ref.shape, 0) // rows_per_head
        col_head = lax.broadcasted_iota(jnp.int32, bias_ref.shape, 1) % HEADS_PER_STEP
        bias_ref[...] = jnp.where(row_head == col_head, 0.0, NEG_BIG)

    def update(kblk, vblk, bias):
        s = lax.dot_general(qm_ref[0, 0], kblk, (((1,), (1,)), ((), ())), preferred_element_type=F32)
        s = s + bias
        m_old = m_ref[...]
        m_new = jnp.maximum(m_old, jnp.max(s, axis=-1, keepdims=True))
        alpha = jnp.exp2(m_old - m_new)
        p = jnp.exp2(s - jnp.tile(m_new, (1, s.shape[1] // LANES)))
        l_ref[...] = alpha * l_ref[...] + jnp.sum(p, axis=-1, keepdims=True)
        m_ref[...] = m_new
        acc_ref[...] = (jnp.tile(alpha, (1, VALUE_DIM // LANES)) * acc_ref[...]
                        + jnp.dot(p.astype(BF16), vblk, preferred_element_type=F32))

    kblk = jnp.concatenate([r[0].reshape(page_rows, VALUE_DIM).astype(BF16) for r in kc], axis=0)
    vblk = jnp.concatenate([r[0].reshape(page_rows, VALUE_DIM).astype(BF16) for r in vc], axis=0)
    update(kblk, vblk, bias_ref[...])

    @pl.when(step == last)
    def _():
        shape = (n_rows, LANES)
        row = lax.broadcasted_iota(jnp.int32, shape, 0)
        col = lax.broadcasted_iota(jnp.int32, shape, 1)
        keep = jnp.logical_and(row // rows_per_head == col % HEADS_PER_STEP,
                               col // HEADS_PER_STEP <= row % n_tok)
        update(kn_ref[0, 0], vn_ref[0, 0], jnp.where(keep, 0.0, NEG_BIG))
        lam = _lambda_value(lq1_ref[...], lk1_ref[...], lq2_ref[...], lk2_ref[...], lambda_init)
        on = acc_ref[...] / jnp.tile(l_ref[...], (1, VALUE_DIM // LANES))
        for h in range(HEADS_PER_STEP):
            r0 = h * rows_per_head
            o = on[r0:r0 + n_tok, :] - lam * on[r0 + n_tok:r0 + 2 * n_tok, :]
            cols = slice(h * VALUE_DIM, (h + 1) * VALUE_DIM)
            o_ref[0, :, cols] = _subln_gate(o, sg_ref[...], lambda_init, sz_ref[0, :, cols]).astype(o_ref.dtype)


def _decode_attention(q, k_new, v_new, sz, cache_k, cache_v, page_off, page_table, subln_g, lam_vecs,
                      lambda_init, bsz, n_tok):
    n_pages = page_table.shape[1]
    pps = PAGES_PER_STEP
    hps = HEADS_PER_STEP
    n_hb = N_HEADS // hps
    assert n_pages % pps == 0 and n_tok * hps <= LANES
    rows = hps * 2 * n_tok
    q6 = q.reshape(bsz, n_tok, n_hb, hps, 2, HEAD_DIM)
    eye = jnp.eye(2, dtype=BF16)
    qm = jnp.einsum('btnhcd,ck->bnhctkd', q6, eye).reshape(bsz, n_hb, rows, VALUE_DIM)

    def new_rows(x):
        x = x.reshape(bsz, n_tok, n_hb, hps, VALUE_DIM).transpose(0, 2, 1, 3, 4)
        x = x.reshape(bsz, n_hb, n_tok * hps, VALUE_DIM)
        return jnp.pad(x, ((0, 0), (0, 0), (0, LANES - n_tok * hps), (0, 0)))

    def page_spec(r):
        return pl.BlockSpec((1, PAGE_SIZE, hps, VALUE_DIM),
                            lambda b, hb, s, pt: (page_off + pt[b, s * pps + r], 0, hb, 0))
    newspec = pl.BlockSpec((1, 1, LANES, VALUE_DIM), lambda b, hb, s, pt: (b, hb, 0, 0))
    tokspec = pl.BlockSpec((1, n_tok, hps * VALUE_DIM), lambda b, hb, s, pt: (b, 0, hb))
    vec = pl.BlockSpec((1, HEAD_DIM), lambda b, hb, s, pt: (0, 0))
    grid_spec = pltpu.PrefetchScalarGridSpec(
        num_scalar_prefetch=1,
        grid=(bsz, n_hb, n_pages // pps),
        in_specs=[pl.BlockSpec((1, 1, rows, VALUE_DIM), lambda b, hb, s, pt: (b, hb, 0, 0))]
                 + [page_spec(r) for r in range(pps)] + [page_spec(r) for r in range(pps)]
                 + [newspec, newspec, tokspec,
                    pl.BlockSpec((1, VALUE_DIM), lambda b, hb, s, pt: (0, 0)), vec, vec, vec, vec],
        out_specs=tokspec,
        scratch_shapes=[pltpu.VMEM((rows, LANES), F32), pltpu.VMEM((rows, LANES), F32),
                        pltpu.VMEM((rows, VALUE_DIM), F32),
                        pltpu.VMEM((rows, pps * PAGE_SIZE * hps), F32)],
    )
    out = pl.pallas_call(
        functools.partial(_decode_kernel, lambda_init, n_tok),
        grid_spec=grid_spec,
        out_shape=jax.ShapeDtypeStruct((bsz, n_tok, D_MODEL), BF16),
        compiler_params=_cparams(("parallel", "parallel", "arbitrary")),
        name="decode_diff_attention",
    )(page_table, qm, *([cache_k] * pps), *([cache_v] * pps),
      new_rows(k_new), new_rows(v_new),
      sz.reshape(bsz, n_tok, D_MODEL), subln_g.reshape(1, VALUE_DIM),
      *[x.reshape(1, HEAD_DIM) for x in lam_vecs])
    return out.reshape(bsz * n_tok, D_MODEL)


def _lambda_init(layer):
    return 0.8 - 0.6 * math.exp(-0.3 * layer)


def _s5_layer(h, xn, w_in, w_glu, b_glu, w_out, prep, d, h0r, h0i, bsz, seq, seg_len):
    u, = _matmul(xn, w_in, 0, D_MODEL, _epi_f32, [], [F32], "s5_in_u")
    sz, = _matmul(xn, w_in, D_MODEL, D_MODEL, _epi_silu, [], [F32], "s5_in_z")
    g32, g16, hr, hi = _s5_scan(u.reshape(bsz, seq, D_MODEL), prep, d, h0r, h0i, seg_len)
    g32 = g32.reshape(bsz * seq, D_MODEL)
    g16 = g16.reshape(bsz * seq, D_MODEL)
    m, = _matmul(g16, w_glu, 0, D_MODEL, _epi_glu,
                 [(g32, "mn"), (sz, "mn"), (b_glu.reshape(1, D_MODEL), "n")], [BF16], "s5_glu", tn=512)
    h_new, = _matmul(m, w_out, 0, D_MODEL, _epi_residual, [(h, "mn")], [F32], "s5_out")
    shape = (bsz, S5_GROUPS, S5_STATE)
    return h_new, hr.reshape(shape), hi.reshape(shape)


def _attn_in_proj(xn, w_in):
    q, = _matmul(xn, w_in, 0, D_MODEL, _epi_scaled_bf16, [], [BF16], "attn_in_q")
    k32, k16 = _matmul(xn, w_in, D_MODEL, D_MODEL, _epi_f32_bf16, [], [F32, BF16], "attn_in_k")
    v32, v16 = _matmul(xn, w_in, 2 * D_MODEL, D_MODEL, _epi_f32_bf16, [], [F32, BF16], "attn_in_v")
    sz, = _matmul(xn, w_in, 3 * D_MODEL, D_MODEL, _epi_silu, [], [F32], "attn_in_z")
    return q, k32, k16, v32, v16, sz


def kernel(x_prompt, x_sample, cache_k, cache_v, state_s5_re, state_s5_im, page_table, norm_g, final_norm_g, s5_w_in, s5_lambda_re, s5_lambda_im, s5_log_step, s5_b_re, s5_b_im, s5_c_re, s5_c_im, s5_d, s5_w_glu, s5_b_glu, s5_w_out, attn_w_in, attn_lam_q1, attn_lam_k1, attn_lam_q2, attn_lam_k2, attn_subln_g, attn_w_out):
    bp, lp, _ = x_prompt.shape
    bs, ls, _ = x_sample.shape
    depth = norm_g.shape[0]
    n_pool = cache_k.shape[1]
    seg_len = min(128, lp // 2)

    hp = x_prompt.reshape(bp * lp, D_MODEL)
    hs = x_sample.reshape(bs * ls, D_MODEL)
    k_p, v_p, k_s, v_s = [], [], [], []
    sr_p, si_p, sr_s, si_s = [], [], [], []
    for layer in range(depth):
        idx = layer // 2
        xp = _rmsnorm(hp, norm_g[layer], BF16)
        xs = _rmsnorm(hs, norm_g[layer], BF16)
        if layer % 2 == 0:
            prep = _s5_prepare(s5_lambda_re[idx], s5_lambda_im[idx], s5_log_step[idx],
                               s5_b_re[idx], s5_b_im[idx], s5_c_re[idx], s5_c_im[idx], seg_len)
            w_in = s5_w_in[idx].astype(BF16)
            w_glu = s5_w_glu[idx].astype(BF16)
            w_out = s5_w_out[idx].astype(BF16)
            d = s5_d[idx].reshape(1, D_MODEL)
            hp, hr, hi = _s5_layer(hp, xp, w_in, w_glu, s5_b_glu[idx], w_out, prep, d,
                                   None, None, bp, lp, seg_len)
            sr_p.append(hr); si_p.append(hi)
            h0r = state_s5_re[idx].reshape(bs, S5_GROUPS * S5_STATE)
            h0i = state_s5_im[idx].reshape(bs, S5_GROUPS * S5_STATE)
            hs, hr, hi = _s5_layer(hs, xs, w_in, w_glu, s5_b_glu[idx], w_out, prep, d,
                                   h0r, h0i, bs, ls, ls)
            sr_s.append(hr); si_s.append(hi)
        else:
            lam0 = _lambda_init(layer)
            lam_vecs = (attn_lam_q1[idx], attn_lam_k1[idx], attn_lam_q2[idx], attn_lam_k2[idx])
            w_in = attn_w_in[idx].astype(BF16)
            w_out = attn_w_out[idx].astype(BF16)
            q, k32, k16, v32, v16, sz = _attn_in_proj(xp, w_in)
            m = _flash_attention(q, k16, v16, sz, attn_subln_g[idx], lam_vecs, lam0, bp, lp)
            hp, = _matmul(m, w_out, 0, D_MODEL, _epi_residual, [(hp, "mn")], [F32], "attn_out")
            k_p.append(k32.reshape(bp, lp, N_HEADS, VALUE_DIM))
            v_p.append(v32.reshape(bp, lp, N_HEADS, VALUE_DIM))
            q, k32, k16, v32, v16, sz = _attn_in_proj(xs, w_in)
            m = _decode_attention(q, k16, v16, sz,
                                  cache_k.reshape(-1, PAGE_SIZE, N_HEADS, VALUE_DIM),
                                  cache_v.reshape(-1, PAGE_SIZE, N_HEADS, VALUE_DIM),
                                  idx * n_pool, page_table, attn_subln_g[idx], lam_vecs, lam0, bs, ls)
            hs, = _matmul(m, w_out, 0, D_MODEL, _epi_residual, [(hs, "mn")], [F32], "attn_out")
            k_s.append(k32.reshape(bs, ls, N_HEADS, VALUE_DIM))
            v_s.append(v32.reshape(bs, ls, N_HEADS, VALUE_DIM))
    y_prompt = _rmsnorm(hp, final_norm_g, F32).reshape(bp, lp, D_MODEL)
    y_sample = _rmsnorm(hs, final_norm_g, F32).reshape(bs, ls, D_MODEL)
    return (y_prompt, y_sample, jnp.stack(k_p), jnp.stack(v_p), jnp.stack(k_s), jnp.stack(v_s),
            jnp.stack(sr_p), jnp.stack(si_p), jnp.stack(sr_s), jnp.stack(si_s))
```

```python
import functools
import math

import jax
import jax.numpy as jnp
from jax import lax
from jax.experimental import pallas as pl
from jax.experimental.pallas import tpu as pltpu

F32 = jnp.float32
BF16 = jnp.bfloat16

D_MODEL = 4096
S5_GROUP = 16
S5_STATE = 64
S5_GROUPS = D_MODEL // S5_GROUP
GROUPS_PER_BLOCK = 16
N_GROUP_BLOCKS = S5_GROUPS // GROUPS_PER_BLOCK
BLOCK_CH = GROUPS_PER_BLOCK * S5_GROUP
BLOCK_ST = GROUPS_PER_BLOCK * S5_STATE
HEAD_DIM = 128
VALUE_DIM = 2 * HEAD_DIM
N_HEADS = D_MODEL // VALUE_DIM
PAGE_SIZE = 128
EPS = 1e-6
ATTN_SCALE = 1.0 / math.sqrt(HEAD_DIM)
Q_SCALE = ATTN_SCALE * math.log2(math.e)
NEG_BIG = -1e30
SUBLANES = 8
LANES = 128
FLASH_TQ = 512
FLASH_TK = 512
VMEM_LIMIT = 56 * 1024 * 1024


def _cparams(sem):
    return pltpu.CompilerParams(dimension_semantics=sem, vmem_limit_bytes=VMEM_LIMIT)


def _rmsnorm_kernel(x_ref, g_ref, o_ref):
    x = x_ref[...]
    ms = jnp.mean(x * x, axis=-1, keepdims=True)
    o_ref[...] = (x * lax.rsqrt(ms + EPS) * g_ref[...]).astype(o_ref.dtype)


def _rmsnorm(x, g, out_dtype):
    m, d = x.shape
    tm = min(m, 256)
    return pl.pallas_call(
        _rmsnorm_kernel,
        grid=(m // tm,),
        in_specs=[pl.BlockSpec((tm, d), lambda i: (i, 0)),
                  pl.BlockSpec((1, d), lambda i: (0, 0))],
        out_specs=pl.BlockSpec((tm, d), lambda i: (i, 0)),
        out_shape=jax.ShapeDtypeStruct((m, d), out_dtype),
        compiler_params=_cparams(("parallel",)),
        name="rmsnorm",
    )(x, g.reshape(1, d))


class _Rows:
    def __init__(self, lhs, extras=(), stack=None):
        self.lhs, self.extras, self.stack = lhs, list(extras), stack


def _mm_kernel(epilogue, n_in_big, n_in_small, n_out, stacked, a_ref, w_ref, *refs):
    w = w_ref[0].astype(BF16)
    in_big = refs[:n_in_big]
    a_small = refs[n_in_big]
    in_small = refs[n_in_big + 1:n_in_big + 1 + n_in_small]
    outs = list(refs[n_in_big + 1 + n_in_small:])
    out_big, out_small = outs[:n_out], outs[n_out:]
    if stacked:
        out_big[0] = out_big[0].at[0]
        out_small[0] = out_small[0].at[0]
    epilogue(jnp.dot(a_ref[...], w, preferred_element_type=F32), in_big, out_big)

    @pl.when(pl.program_id(0) == pl.num_programs(0) - 1)
    def _():
        epilogue(jnp.dot(a_small[...], w, preferred_element_type=F32), in_small, out_small)


def _matmul(big, small, w, layer, col0, ncols, epilogue, outs, name, tn=512):
    m, k = big.lhs.shape
    ms = small.lhs.shape[0]
    tm = min(m, 1024)
    tn = min(ncols, tn)
    assert m % tm == 0 and ncols % tn == 0 and col0 % tn == 0
    cb0 = col0 // tn
    n_i = m // tm

    def small_col(i, j):
        return jnp.where(i == n_i - 1, j, 0)

    in_specs = [pl.BlockSpec((tm, k), lambda i, j: (i, 0)),
                pl.BlockSpec((1, k, tn), lambda i, j: (layer, 0, cb0 + j))]
    args = [big.lhs, w]
    out_specs, out_shape, aliases = [], [], {}
    n_in = []
    for rows, nrow, tile, colmap in ((big, m, tm, lambda i, j: j), (small, ms, ms, small_col)):
        rowmap = (lambda i, j: i) if rows is big else (lambda i, j: 0)
        n0 = len(args)
        if rows is small:
            in_specs.append(pl.BlockSpec((ms, k), lambda i, j: (0, 0)))
            args.append(small.lhs)
            n0 += 1
        for arr, kind in rows.extras:
            if kind == "mn":
                in_specs.append(pl.BlockSpec((tile, tn), lambda i, j, r=rowmap, c=colmap: (r(i, j), c(i, j))))
            else:
                in_specs.append(pl.BlockSpec((1, tn), lambda i, j: (0, j)))
            args.append(arr)
        specs = [pl.BlockSpec((tile, tn), lambda i, j, r=rowmap, c=colmap: (r(i, j), c(i, j))) for _ in outs]
        shapes = [jax.ShapeDtypeStruct((nrow, ncols), dt) for dt in outs]
        if rows.stack is not None:
            slot, buf = rows.stack
            assert buf.shape[1:] == (nrow, ncols) and buf.dtype == outs[0]
            specs[0] = pl.BlockSpec((1, tile, tn), lambda i, j, r=rowmap, c=colmap: (slot, r(i, j), c(i, j)))
            shapes[0] = jax.ShapeDtypeStruct(buf.shape, outs[0])
            in_specs.append(pl.BlockSpec(memory_space=pl.ANY))
            args.append(buf)
            aliases[len(args) - 1] = len(out_specs)
        n_in.append(len(args) - n0)
        out_specs += specs
        out_shape += shapes
    assert (big.stack is None) == (small.stack is None)
    res = pl.pallas_call(
        functools.partial(_mm_kernel, epilogue, n_in[0], n_in[1], len(outs), big.stack is not None),
        grid=(n_i, ncols // tn),
        in_specs=in_specs,
        out_specs=out_specs,
        out_shape=out_shape,
        input_output_aliases=aliases,
        compiler_params=_cparams(("arbitrary", "arbitrary")),
        name=name,
    )(*args)
    return res[:len(outs)], res[len(outs):]


def _epi_f32_bf16(acc, extra, outs):
    outs[0][...] = acc
    outs[1][...] = acc.astype(BF16)


def _epi_f32(acc, extra, outs):
    outs[0][...] = acc


def _epi_silu(acc, extra, outs):
    outs[0][...] = acc * jax.nn.sigmoid(acc)


def _epi_scaled_bf16(acc, extra, outs):
    outs[0][...] = (acc * Q_SCALE).astype(BF16)


def _epi_residual(acc, extra, outs):
    outs[0][...] = extra[0][...] + acc


def _epi_glu(acc, extra, outs):
    g = extra[0][...]
    gate = jax.nn.sigmoid(acc + extra[2][...])
    outs[0][...] = (g * gate * extra[1][...]).astype(BF16)


def _s5_disc_kernel(lr_ref, li_ref, ls_ref, br_ref, bi_ref, ar_ref, ai_ref, bbr_ref, bbi_ref):
    step = jnp.exp(ls_ref[...])
    lr = lr_ref[...]
    li = li_ref[...]
    mag = jnp.exp(lr * step)
    ar = mag * jnp.cos(li * step)
    ai = mag * jnp.sin(li * step)
    den = lr * lr + li * li
    nr = ar - 1.0
    ni = ai
    fr = (nr * lr + ni * li) / den
    fi = (ni * lr - nr * li) / den
    br = br_ref[...]
    bi = bi_ref[...]
    ar_ref[...] = ar
    ai_ref[...] = ai
    bbr_ref[...] = fr * br - fi * bi
    bbi_ref[...] = fr * bi + fi * br


def _s5_pow_kernel(lr_ref, li_ref, st_ref, pr_ref, pi_ref):
    rows = pr_ref.shape[0]
    kk = (lax.broadcasted_iota(jnp.int32, (rows, 1), 0) + 1).astype(F32)
    st = jnp.exp(st_ref[...])
    ph_r = lr_ref[...] * st
    ph_i = li_ref[...] * st
    mag = jnp.exp(ph_r * kk)
    pr_ref[...] = mag * jnp.cos(ph_i * kk)
    pi_ref[...] = mag * jnp.sin(ph_i * kk)


def _s5_prepare(lam_re, lam_im, log_step, b_re, b_im, c_re, c_im, seg_len):
    g, p, c = S5_GROUPS, S5_STATE, S5_GROUP
    cp = c * p
    lr_b = jnp.broadcast_to(lam_re[:, None, :], (g, c, p)).reshape(g, cp)
    li_b = jnp.broadcast_to(lam_im[:, None, :], (g, c, p)).reshape(g, cp)
    br_t = jnp.swapaxes(b_re, 1, 2).reshape(g, cp)
    bi_t = jnp.swapaxes(b_im, 1, 2).reshape(g, cp)
    full = pl.BlockSpec((g, cp), lambda: (0, 0))
    ar_b, ai_b, bbr_t, bbi_t = pl.pallas_call(
        _s5_disc_kernel,
        in_specs=[full, full, pl.BlockSpec((g, 1), lambda: (0, 0)), full, full],
        out_specs=[full] * 4,
        out_shape=[jax.ShapeDtypeStruct((g, cp), F32)] * 4,
        name="s5_discretize",
    )(lr_b, li_b, log_step.reshape(g, 1), br_t, bi_t)
    a_re = ar_b[:, :p].reshape(1, g * p)
    a_im = ai_b[:, :p].reshape(1, g * p)

    nb, gl = N_GROUP_BLOCKS, GROUPS_PER_BLOCK
    eye = jnp.eye(gl, dtype=F32)
    def in_proj(bb):
        return jnp.einsum('bgcp,gh->bgchp', bb.reshape(nb, gl, c, p), eye).reshape(nb, gl * c, gl * p)
    wb = jnp.concatenate([in_proj(bbr_t), in_proj(bbi_t)], axis=-1).astype(BF16)
    def out_proj(cc):
        return jnp.einsum('bgcp,gh->bgphc', cc.reshape(nb, gl, c, p), eye).reshape(nb, gl * p, gl * c)
    wc = jnp.concatenate([out_proj(c_re), out_proj(-c_im)], axis=1).astype(BF16)

    n = g * p
    row = pl.BlockSpec((1, n), lambda: (0, 0))
    tab = pl.BlockSpec((seg_len, n), lambda: (0, 0))
    st_b = jnp.broadcast_to(log_step[:, None], (g, p)).reshape(1, n)
    p_re, p_im = pl.pallas_call(
        _s5_pow_kernel,
        in_specs=[row, row, row],
        out_specs=[tab, tab],
        out_shape=[jax.ShapeDtypeStruct((seg_len, n), F32)] * 2,
        name="s5_powers",
    )(lam_re.reshape(1, n), lam_im.reshape(1, n), st_b)
    return dict(a_re=a_re, a_im=a_im, wb=wb, wc=wc, p_re=p_re, p_im=p_im)


def _s5_scan_kernel(two_seg, seg_len, *refs):
    if two_seg:
        (u_ref, wb_ref, wc_ref, ar_ref, ai_ref, d_ref, pr_ref, pi_ref,
         g32_ref, g16_ref, hr_out, hi_out, up_ref, bu_ref, hb_ref, carry_ref) = refs
    else:
        (u_ref, wb_ref, wc_ref, ar_ref, ai_ref, d_ref, h0r_ref, h0i_ref,
         g32_ref, g16_ref, hr_out, hi_out, up_ref, bu_ref, hb_ref, carry_ref) = refs
    n_chain = SUBLANES
    rows = seg_len * n_chain
    ns = BLOCK_ST
    i = pl.program_id(1)
    last = pl.num_programs(1) - 1

    @pl.when(i == 0)
    def _():
        if two_seg:
            carry_ref[...] = jnp.zeros_like(carry_ref)
        else:
            carry_ref[:, :ns] = h0r_ref[...]
            carry_ref[:, ns:] = h0i_ref[...]

    for j in range(n_chain):
        if two_seg:
            b, s = j % 4, j // 4
            up_ref[:, j, :] = u_ref[b, s * seg_len:(s + 1) * seg_len, :]
        else:
            up_ref[:, j, :] = u_ref[j, :, :]
    upf = up_ref[...].reshape(rows, BLOCK_CH)
    bu_ref[...] = jnp.dot(upf.astype(BF16), wb_ref[0], preferred_element_type=F32)

    ar = jnp.broadcast_to(ar_ref[...], (n_chain, ns))
    ai = jnp.broadcast_to(ai_ref[...], (n_chain, ns))

    def step(k, carry):
        hr, hi = carry
        r = pl.multiple_of(k * n_chain, n_chain)
        nhr = ar * hr - ai * hi + bu_ref[pl.ds(r, n_chain), :ns]
        nhi = ar * hi + ai * hr + bu_ref[pl.ds(r, n_chain), ns:]
        bu_ref[pl.ds(r, n_chain), :ns] = nhr
        bu_ref[pl.ds(r, n_chain), ns:] = nhi
        return nhr, nhi

    hr, hi = lax.fori_loop(0, seg_len, step, (carry_ref[:, :ns], carry_ref[:, ns:]),
                           unroll=min(seg_len, 4))

    if two_seg:
        second_half = lax.broadcasted_iota(jnp.int32, (n_chain, ns), 0) >= n_chain // 2
        hm_r = jnp.where(second_half, pltpu.roll(hr, n_chain // 2, 0), 0.0)
        hm_i = jnp.where(second_half, pltpu.roll(hi, n_chain // 2, 0), 0.0)

        def power_row(k):
            def bcast(ref):
                return jnp.concatenate([jnp.broadcast_to(ref[k, cb:cb + 1, :], (n_chain, LANES))
                                        for cb in range(ns // LANES)], axis=1)
            return bcast(pr_ref), bcast(pi_ref)

        def fix(k2, _):
            r = pl.multiple_of(k2 * 2 * n_chain, 2 * n_chain)
            outs_r, outs_i = [], []
            for h in range(2):
                rr = r + h * n_chain
                pr, pi = power_row(k2 * 2 + h)
                outs_r.append(bu_ref[pl.ds(rr, n_chain), :ns] + (pr * hm_r - pi * hm_i))
                outs_i.append(bu_ref[pl.ds(rr, n_chain), ns:] + (pr * hm_i + pi * hm_r))
            hb_ref[pl.ds(r, 2 * n_chain), :ns] = jnp.concatenate(outs_r, axis=0).astype(BF16)
            hb_ref[pl.ds(r, 2 * n_chain), ns:] = jnp.concatenate(outs_i, axis=0).astype(BF16)
            return 0

        lax.fori_loop(0, seg_len // 2, fix, 0, unroll=2)
        pr, pi = power_row(seg_len - 1)
        end_r = hr + (pr * hm_r - pi * hm_i)
        end_i = hi + (pr * hm_i + pi * hm_r)
        first_half = jnp.logical_not(second_half)
        carry_ref[:, :ns] = jnp.where(first_half, pltpu.roll(end_r, n_chain // 2, 0), 0.0)
        carry_ref[:, ns:] = jnp.where(first_half, pltpu.roll(end_i, n_chain // 2, 0), 0.0)

        @pl.when(i == last)
        def _():
            hr_out[...] = end_r[n_chain // 2:, :]
            hi_out[...] = end_i[n_chain // 2:, :]
    else:
        hb_ref[...] = bu_ref[...].astype(BF16)
        carry_ref[:, :ns] = hr
        carry_ref[:, ns:] = hi

        @pl.when(i == last)
        def _():
            hr_out[...] = hr
            hi_out[...] = hi

    y = jnp.dot(hb_ref[...], wc_ref[0], preferred_element_type=F32)
    y = y + d_ref[...] * upf
    gl = jax.nn.gelu(y)
    up_ref[...] = gl.reshape(seg_len, n_chain, BLOCK_CH)
    for j in range(n_chain):
        v = up_ref[:, j, :]
        if two_seg:
            b, s = j % 4, j // 4
            g32_ref[b, s * seg_len:(s + 1) * seg_len, :] = v
        else:
            g32_ref[j, :, :] = v
    g16_ref[...] = g32_ref[...].astype(BF16)


def _s5_scan(u, prep, d, h0r, h0i, seg_len):
    bsz, seq, _ = u.shape
    two_seg = h0r is None
    n_st = S5_GROUPS * S5_STATE
    if two_seg:
        assert bsz == SUBLANES // 2 and seq % (2 * seg_len) == 0 and seg_len % 16 == 0
        t_tile = 2 * seg_len
    else:
        assert bsz == SUBLANES and seq == seg_len
        t_tile = seg_len
    nt = seq // t_tile
    rows = seg_len * SUBLANES
    ublock = pl.BlockSpec((bsz, t_tile, BLOCK_CH), lambda gb, i: (0, i, gb))
    strow = pl.BlockSpec((1, BLOCK_ST), lambda gb, i: (0, gb))
    in_specs = [ublock,
                pl.BlockSpec((1, BLOCK_CH, 2 * BLOCK_ST), lambda gb, i: (gb, 0, 0)),
                pl.BlockSpec((1, 2 * BLOCK_ST, BLOCK_CH), lambda gb, i: (gb, 0, 0)),
                strow, strow,
                pl.BlockSpec((1, BLOCK_CH), lambda gb, i: (0, gb))]
    args = [u, prep["wb"], prep["wc"], prep["a_re"], prep["a_im"], d]
    if two_seg:
        tab = pl.BlockSpec((seg_len, BLOCK_ST // LANES, LANES), lambda gb, i: (0, gb, 0))
        in_specs += [tab, tab]
        args += [prep["p_re"].reshape(seg_len, -1, LANES), prep["p_im"].reshape(seg_len, -1, LANES)]
    else:
        st = pl.BlockSpec((bsz, BLOCK_ST), lambda gb, i: (0, gb))
        in_specs += [st, st]
        args += [h0r, h0i]
    stout = pl.BlockSpec((bsz, BLOCK_ST), lambda gb, i: (0, gb))
    return pl.pallas_call(
        functools.partial(_s5_scan_kernel, two_seg, seg_len),
        grid=(N_GROUP_BLOCKS, nt),
        in_specs=in_specs,
        out_specs=[ublock, ublock, stout, stout],
        out_shape=[jax.ShapeDtypeStruct(u.shape, F32), jax.ShapeDtypeStruct(u.shape, BF16),
                   jax.ShapeDtypeStruct((bsz, n_st), F32), jax.ShapeDtypeStruct((bsz, n_st), F32)],
        scratch_shapes=[pltpu.VMEM((seg_len, SUBLANES, BLOCK_CH), F32),
                        pltpu.VMEM((rows, 2 * BLOCK_ST), F32),
                        pltpu.VMEM((rows, 2 * BLOCK_ST), BF16),
                        pltpu.VMEM((SUBLANES, 2 * BLOCK_ST), F32)],
        compiler_params=_cparams(("parallel", "arbitrary")),
        name="s5_scan_prompt" if two_seg else "s5_scan_sample",
    )(*args)


def _lambda_value(lq1, lk1, lq2, lk2, lambda_init):
    return (jnp.exp(jnp.sum(lq1 * lk1, axis=-1, keepdims=True))
            - jnp.exp(jnp.sum(lq2 * lk2, axis=-1, keepdims=True)) + lambda_init)


def _subln_gate(o, subln_g, lambda_init, sz):
    ms = jnp.mean(o * o, axis=-1, keepdims=True)
    return (o * lax.rsqrt(ms + EPS) * subln_g) * (1.0 - lambda_init) * sz


def _flash_kernel(lambda_init, tq, tk, q_ref, k_ref, v_ref, sz_ref, sg_ref,
                  lq1_ref, lk1_ref, lq2_ref, lk2_ref, o_ref, m_ref, l_ref, acc_ref):
    qi = pl.program_id(2)
    ratio = tq // tk
    first_diag = qi * ratio
    m_ref[...] = jnp.full_like(m_ref, NEG_BIG)
    l_ref[...] = jnp.zeros_like(l_ref)
    acc_ref[...] = jnp.zeros_like(acc_ref)

    def update(ki, masked):
        r = pl.multiple_of(ki * tk, tk)
        v = v_ref[pl.ds(r, tk), :]
        if masked:
            keep = (lax.broadcasted_iota(jnp.int32, (tq, tk), 1) + ki * tk
                    <= lax.broadcasted_iota(jnp.int32, (tq, tk), 0) + qi * tq)
        for c in range(2):
            q = q_ref[:, c * HEAD_DIM:(c + 1) * HEAD_DIM]
            k = k_ref[pl.ds(r, tk), c * HEAD_DIM:(c + 1) * HEAD_DIM]
            s = lax.dot_general(q, k, (((1,), (1,)), ((), ())), preferred_element_type=F32)
            if masked:
                s = jnp.where(keep, s, NEG_BIG)
            m_old = m_ref[c]
            m_new = jnp.maximum(m_old, jnp.max(s, axis=-1, keepdims=True))
            alpha = jnp.exp2(m_old - m_new)
            p = jnp.exp2(s - jnp.tile(m_new, (1, tk // LANES)))
            l_ref[c] = alpha * l_ref[c] + jnp.sum(p, axis=-1, keepdims=True)
            acc_ref[c] = (jnp.tile(alpha, (1, VALUE_DIM // LANES)) * acc_ref[c]
                          + jnp.dot(p.astype(BF16), v, preferred_element_type=F32))
            m_ref[c] = m_new

    def visible_block(ki, carry):
        update(ki, False)
        return carry

    lax.fori_loop(0, first_diag, visible_block, 0)
    for d in range(ratio):
        update(first_diag + d, True)

    lam = _lambda_value(lq1_ref[...], lk1_ref[...], lq2_ref[...], lk2_ref[...], lambda_init)
    rep = VALUE_DIM // LANES
    o = (acc_ref[0] / jnp.tile(l_ref[0], (1, rep))
         - lam * (acc_ref[1] / jnp.tile(l_ref[1], (1, rep))))
    o_ref[...] = _subln_gate(o, sg_ref[...], lambda_init, sz_ref[...]).astype(o_ref.dtype)


def _flash_attention(q, k, v, sz, subln_g, lam_vecs, lambda_init, bsz, seq):
    tq = min(seq, FLASH_TQ)
    tk = min(seq, FLASH_TK)
    assert seq % tq == 0 and tq % tk == 0
    nq = seq // tq
    qspec = pl.BlockSpec((tq, VALUE_DIM), lambda b, h, qi: (b * nq + qi, h))
    kspec = pl.BlockSpec((seq, VALUE_DIM), lambda b, h, qi: (b, h))
    vec = pl.BlockSpec((1, HEAD_DIM), lambda b, h, qi: (0, 0))
    return pl.pallas_call(
        functools.partial(_flash_kernel, lambda_init, tq, tk),
        grid=(bsz, N_HEADS, nq),
        in_specs=[qspec, kspec, kspec, qspec,
                  pl.BlockSpec((1, VALUE_DIM), lambda b, h, qi: (0, 0)),
                  vec, vec, vec, vec],
        out_specs=qspec,
        out_shape=jax.ShapeDtypeStruct((bsz * seq, D_MODEL), BF16),
        scratch_shapes=[pltpu.VMEM((2, tq, LANES), F32), pltpu.VMEM((2, tq, LANES), F32),
                        pltpu.VMEM((2, tq, VALUE_DIM), F32)],
        compiler_params=_cparams(("parallel", "parallel", "arbitrary")),
        name="flash_diff_attention",
    )(q, k, v, sz, subln_g.reshape(1, VALUE_DIM), *[x.reshape(1, HEAD_DIM) for x in lam_vecs])


HEADS_PER_STEP = SUBLANES
PAGES_PER_STEP = 4


def _decode_kernel(lambda_init, n_tok, pt_ref, qm_ref, *refs):
    pps = PAGES_PER_STEP
    kc = refs[:pps]
    vc = refs[pps:2 * pps]
    (kn_ref, vn_ref, sz_ref, sg_ref, lq1_ref, lk1_ref, lq2_ref, lk2_ref,
     o_ref, m_ref, l_ref, acc_ref, bias_ref) = refs[2 * pps:]
    step = pl.program_id(2)
    last = pl.num_programs(2) - 1
    rows_per_head = 2 * n_tok
    n_rows = HEADS_PER_STEP * rows_per_head
    page_rows = PAGE_SIZE * HEADS_PER_STEP

    @pl.when(step == 0)
    def _():
        m_ref[...] = jnp.full_like(m_ref, NEG_BIG)
        l_ref[...] = jnp.zeros_like(l_ref)
        acc_ref[...] = jnp.zeros_like(acc_ref)
        row_head = lax.broadcasted_iota(jnp.int32, bias_ref.shape, 0) // rows_per_head
        col_head = lax.broadcasted_iota(jnp.int32, bias_ref.shape, 1) % HEADS_PER_STEP
        bias_ref[...] = jnp.where(row_head == col_head, 0.0, NEG_BIG)

    def update(kblk, vblk, bias):
        s = lax.dot_general(qm_ref[0, 0], kblk, (((1,), (1,)), ((), ())), preferred_element_type=F32)
        s = s + bias
        m_old = m_ref[...]
        m_new = jnp.maximum(m_old, jnp.max(s, axis=-1, keepdims=True))
        alpha = jnp.exp2(m_old - m_new)
        p = jnp.exp2(s - jnp.tile(m_new, (1, s.shape[1] // LANES)))
        l_ref[...] = alpha * l_ref[...] + jnp.sum(p, axis=-1, keepdims=True)
        m_ref[...] = m_new
        acc_ref[...] = (jnp.tile(alpha, (1, VALUE_DIM // LANES)) * acc_ref[...]
                        + jnp.dot(p.astype(BF16), vblk, preferred_element_type=F32))

    kblk = jnp.concatenate([r[0].reshape(page_rows, VALUE_DIM).astype(BF16) for r in kc], axis=0)
    vblk = jnp.concatenate([r[0].reshape(page_rows, VALUE_DIM).astype(BF16) for r in vc], axis=0)
    update(kblk, vblk, bias_ref[...])

    @pl.when(step == last)
    def _():
        shape = (n_rows, LANES)
        row = lax.broadcasted_iota(jnp.int32, shape, 0)
        col = lax.broadcasted_iota(jnp.int32, shape, 1)
        keep = jnp.logical_and(row // rows_per_head == col % HEADS_PER_STEP,
                               col // HEADS_PER_STEP <= row % n_tok)
        update(kn_ref[0, 0], vn_ref[0, 0], jnp.where(keep, 0.0, NEG_BIG))
        lam = _lambda_value(lq1_ref[...], lk1_ref[...], lq2_ref[...], lk2_ref[...], lambda_init)
        on = acc_ref[...] / jnp.tile(l_ref[...], (1, VALUE_DIM // LANES))
        for h in range(HEADS_PER_STEP):
            r0 = h * rows_per_head
            o = on[r0:r0 + n_tok, :] - lam * on[r0 + n_tok:r0 + 2 * n_tok, :]
            cols = slice(h * VALUE_DIM, (h + 1) * VALUE_DIM)
            o_ref[0, :, cols] = _subln_gate(o, sg_ref[...], lambda_init, sz_ref[0, :, cols]).astype(o_ref.dtype)


def _decode_attention(q, k_new, v_new, sz, cache_k, cache_v, page_off, page_table, subln_g, lam_vecs,
                      lambda_init, bsz, n_tok):
    n_pages = page_table.shape[1]
    pps = PAGES_PER_STEP
    hps = HEADS_PER_STEP
    n_hb = N_HEADS // hps
    assert n_pages % pps == 0 and n_tok * hps <= LANES
    rows = hps * 2 * n_tok
    q6 = q.reshape(bsz, n_tok, n_hb, hps, 2, HEAD_DIM)
    eye = jnp.eye(2, dtype=BF16)
    qm = jnp.einsum('btnhcd,ck->bnhctkd', q6, eye).reshape(bsz, n_hb, rows, VALUE_DIM)

    def new_rows(x):
        x = x.reshape(bsz, n_tok, n_hb, hps, VALUE_DIM).transpose(0, 2, 1, 3, 4)
        x = x.reshape(bsz, n_hb, n_tok * hps, VALUE_DIM)
        return jnp.pad(x, ((0, 0), (0, 0), (0, LANES - n_tok * hps), (0, 0)))

    def page_spec(r):
        return pl.BlockSpec((1, PAGE_SIZE, hps, VALUE_DIM),
                            lambda b, hb, s, pt: (page_off + pt[b, s * pps + r], 0, hb, 0))
    newspec = pl.BlockSpec((1, 1, LANES, VALUE_DIM), lambda b, hb, s, pt: (b, hb, 0, 0))
    tokspec = pl.BlockSpec((1, n_tok, hps * VALUE_DIM), lambda b, hb, s, pt: (b, 0, hb))
    vec = pl.BlockSpec((1, HEAD_DIM), lambda b, hb, s, pt: (0, 0))
    grid_spec = pltpu.PrefetchScalarGridSpec(
        num_scalar_prefetch=1,
        grid=(bsz, n_hb, n_pages // pps),
        in_specs=[pl.BlockSpec((1, 1, rows, VALUE_DIM), lambda b, hb, s, pt: (b, hb, 0, 0))]
                 + [page_spec(r) for r in range(pps)] + [page_spec(r) for r in range(pps)]
                 + [newspec, newspec, tokspec,
                    pl.BlockSpec((1, VALUE_DIM), lambda b, hb, s, pt: (0, 0)), vec, vec, vec, vec],
        out_specs=tokspec,
        scratch_shapes=[pltpu.VMEM((rows, LANES), F32), pltpu.VMEM((rows, LANES), F32),
                        pltpu.VMEM((rows, VALUE_DIM), F32),
                        pltpu.VMEM((rows, pps * PAGE_SIZE * hps), F32)],
    )
    out = pl.pallas_call(
        functools.partial(_decode_kernel, lambda_init, n_tok),
        grid_spec=grid_spec,
        out_shape=jax.ShapeDtypeStruct((bsz, n_tok, D_MODEL), BF16),
        compiler_params=_cparams(("parallel", "parallel", "arbitrary")),
        name="decode_diff_attention",
    )(page_table, qm, *([cache_k] * pps), *([cache_v] * pps),
      new_rows(k_new), new_rows(v_new),
      sz.reshape(bsz, n_tok, D_MODEL), subln_g.reshape(1, VALUE_DIM),
      *[x.reshape(1, HEAD_DIM) for x in lam_vecs])
    return out.reshape(bsz * n_tok, D_MODEL)


def _lambda_init(layer):
    return 0.8 - 0.6 * math.exp(-0.3 * layer)


def _s5_layer(hs2, xn2, idx, w_in, w_glu, b_glu, w_out, prep, d, h0, dims, seg_len):
    (u_p,), (u_s,) = _matmul(_Rows(xn2[0]), _Rows(xn2[1]), w_in, idx, 0, D_MODEL, _epi_f32, [F32], "s5_in_u")
    (sz_p,), (sz_s,) = _matmul(_Rows(xn2[0]), _Rows(xn2[1]), w_in, idx, D_MODEL, D_MODEL, _epi_silu, [F32],
                               "s5_in_z")
    gated, states = [], []
    for u, sz, (bsz, seq), h0ri, sl in ((u_p, sz_p, dims[0], (None, None), seg_len),
                                        (u_s, sz_s, dims[1], h0, dims[1][1])):
        g32, g16, hr, hi = _s5_scan(u.reshape(bsz, seq, D_MODEL), prep, d, h0ri[0], h0ri[1], sl)
        gated.append(_Rows(g16.reshape(bsz * seq, D_MODEL),
                           [(g32.reshape(bsz * seq, D_MODEL), "mn"), (sz, "mn"), (b_glu.reshape(1, D_MODEL), "n")]))
        shape = (bsz, S5_GROUPS, S5_STATE)
        states.append((hr.reshape(shape), hi.reshape(shape)))
    (m_p,), (m_s,) = _matmul(gated[0], gated[1], w_glu, idx, 0, D_MODEL, _epi_glu, [BF16], "s5_glu")
    (h_p,), (h_s,) = _matmul(_Rows(m_p, [(hs2[0], "mn")]), _Rows(m_s, [(hs2[1], "mn")]), w_out, idx, 0, D_MODEL,
                             _epi_residual, [F32], "s5_out")
    return (h_p, h_s), states


def _attn_in_proj(xn2, w_in, idx, k_stacks, v_stacks):
    big, small = _Rows(xn2[0]), _Rows(xn2[1])
    q = _matmul(big, small, w_in, idx, 0, D_MODEL, _epi_scaled_bf16, [BF16], "attn_in_q")
    k = _matmul(_Rows(xn2[0], stack=(idx, k_stacks[0])), _Rows(xn2[1], stack=(idx, k_stacks[1])), w_in, idx,
                D_MODEL, D_MODEL, _epi_f32_bf16, [F32, BF16], "attn_in_k")
    v = _matmul(_Rows(xn2[0], stack=(idx, v_stacks[0])), _Rows(xn2[1], stack=(idx, v_stacks[1])), w_in, idx,
                2 * D_MODEL, D_MODEL, _epi_f32_bf16, [F32, BF16], "attn_in_v")
    sz = _matmul(big, small, w_in, idx, 3 * D_MODEL, D_MODEL, _epi_silu, [F32], "attn_in_z")
    return q, k, v, sz


def kernel(x_prompt, x_sample, cache_k, cache_v, state_s5_re, state_s5_im, page_table, norm_g, final_norm_g, s5_w_in, s5_lambda_re, s5_lambda_im, s5_log_step, s5_b_re, s5_b_im, s5_c_re, s5_c_im, s5_d, s5_w_glu, s5_b_glu, s5_w_out, attn_w_in, attn_lam_q1, attn_lam_k1, attn_lam_q2, attn_lam_k2, attn_subln_g, attn_w_out):
    bp, lp, _ = x_prompt.shape
    bs, ls, _ = x_sample.shape
    depth = norm_g.shape[0]
    n_pool = cache_k.shape[1]
    seg_len = min(128, lp // 2)

    hp = x_prompt.reshape(bp * lp, D_MODEL)
    hs = x_sample.reshape(bs * ls, D_MODEL)
    n_attn = depth // 2
    k_p = jnp.zeros((n_attn, bp * lp, D_MODEL), F32)
    v_p = jnp.zeros((n_attn, bp * lp, D_MODEL), F32)
    k_s = jnp.zeros((n_attn, bs * ls, D_MODEL), F32)
    v_s = jnp.zeros((n_attn, bs * ls, D_MODEL), F32)
    sr_p, si_p, sr_s, si_s = [], [], [], []
    for layer in range(depth):
        idx = layer // 2
        xn2 = (_rmsnorm(hp, norm_g[layer], BF16), _rmsnorm(hs, norm_g[layer], BF16))
        if layer % 2 == 0:
            prep = _s5_prepare(s5_lambda_re[idx], s5_lambda_im[idx], s5_log_step[idx],
                               s5_b_re[idx], s5_b_im[idx], s5_c_re[idx], s5_c_im[idx], seg_len)
            d = s5_d[idx].reshape(1, D_MODEL)
            h0 = (state_s5_re[idx].reshape(bs, S5_GROUPS * S5_STATE),
                  state_s5_im[idx].reshape(bs, S5_GROUPS * S5_STATE))
            (hp, hs), states = _s5_layer((hp, hs), xn2, idx, s5_w_in, s5_w_glu, s5_b_glu[idx], s5_w_out,
                                         prep, d, h0, ((bp, lp), (bs, ls)), seg_len)
            sr_p.append(states[0][0]); si_p.append(states[0][1])
            sr_s.append(states[1][0]); si_s.append(states[1][1])
        else:
            lam0 = _lambda_init(layer)
            lam_vecs = (attn_lam_q1[idx], attn_lam_k1[idx], attn_lam_q2[idx], attn_lam_k2[idx])
            q, k, v, sz = _attn_in_proj(xn2, attn_w_in, idx, (k_p, k_s), (v_p, v_s))
            (k_p, k16_p), (k_s, k16_s) = k
            (v_p, v16_p), (v_s, v16_s) = v
            m_p = _flash_attention(q[0][0], k16_p, v16_p, sz[0][0], attn_subln_g[idx], lam_vecs, lam0, bp, lp)
            m_s = _decode_attention(q[1][0], k16_s, v16_s, sz[1][0],
                                    cache_k.reshape(-1, PAGE_SIZE, N_HEADS, VALUE_DIM),
                                    cache_v.reshape(-1, PAGE_SIZE, N_HEADS, VALUE_DIM),
                                    idx * n_pool, page_table, attn_subln_g[idx], lam_vecs, lam0, bs, ls)
            (hp,), (hs,) = _matmul(_Rows(m_p, [(hp, "mn")]), _Rows(m_s, [(hs, "mn")]), attn_w_out, idx, 0, D_MODEL,
                                   _epi_residual, [F32], "attn_out")
    y_prompt = _rmsnorm(hp, final_norm_g, F32).reshape(bp, lp, D_MODEL)
    y_sample = _rmsnorm(hs, final_norm_g, F32).reshape(bs, ls, D_MODEL)
    kv_p = (n_attn, bp, lp, N_HEADS, VALUE_DIM)
    kv_s = (n_attn, bs, ls, N_HEADS, VALUE_DIM)
    return (y_prompt, y_sample, k_p.reshape(kv_p), v_p.reshape(kv_p), k_s.reshape(kv_s), v_s.reshape(kv_s),
            jnp.stack(sr_p), jnp.stack(si_p), jnp.stack(sr_s), jnp.stack(si_s))
```

```python
import functools
import math

import jax
import jax.numpy as jnp
from jax import lax
from jax.experimental import pallas as pl
from jax.experimental.pallas import tpu as pltpu

F32 = jnp.float32
BF16 = jnp.bfloat16

D_MODEL = 4096
S5_GROUP = 16
S5_STATE = 64
S5_GROUPS = D_MODEL // S5_GROUP
GROUPS_PER_BLOCK = 16
N_GROUP_BLOCKS = S5_GROUPS // GROUPS_PER_BLOCK
BLOCK_CH = GROUPS_PER_BLOCK * S5_GROUP
BLOCK_ST = GROUPS_PER_BLOCK * S5_STATE
HEAD_DIM = 128
VALUE_DIM = 2 * HEAD_DIM
N_HEADS = D_MODEL // VALUE_DIM
PAGE_SIZE = 128
EPS = 1e-6
ATTN_SCALE = 1.0 / math.sqrt(HEAD_DIM)
Q_SCALE = ATTN_SCALE * math.log2(math.e)
NEG_BIG = -1e30
SUBLANES = 8
LANES = 128
FLASH_TQ = 512
FLASH_TK = 512
VMEM_LIMIT = 56 * 1024 * 1024


def _cparams(sem):
    return pltpu.CompilerParams(dimension_semantics=sem, vmem_limit_bytes=VMEM_LIMIT)


def _rmsnorm_kernel(x_ref, g_ref, o_ref):
    x = x_ref[...]
    ms = jnp.mean(x * x, axis=-1, keepdims=True)
    o_ref[...] = (x * lax.rsqrt(ms + EPS) * g_ref[...]).astype(o_ref.dtype)


def _rmsnorm(x, g, out_dtype):
    m, d = x.shape
    tm = min(m, 256)
    return pl.pallas_call(
        _rmsnorm_kernel,
        grid=(m // tm,),
        in_specs=[pl.BlockSpec((tm, d), lambda i: (i, 0)),
                  pl.BlockSpec((1, d), lambda i: (0, 0))],
        out_specs=pl.BlockSpec((tm, d), lambda i: (i, 0)),
        out_shape=jax.ShapeDtypeStruct((m, d), out_dtype),
        compiler_params=_cparams(("parallel",)),
        name="rmsnorm",
    )(x, g.reshape(1, d))


class _Rows:
    def __init__(self, lhs, extras=(), stack=None):
        self.lhs, self.extras, self.stack = lhs, list(extras), stack


def _mm_kernel(epilogue, n_in_big, n_in_small, n_out, stacked, a_ref, w_ref, *refs):
    wbf_ref = refs[-1]
    refs = refs[:-1]
    i = pl.program_id(1)

    @pl.when(i == 0)
    def _():
        wbf_ref[...] = w_ref[0].astype(BF16)

    w = wbf_ref[...]
    in_big = refs[:n_in_big]
    a_small = refs[n_in_big]
    in_small = refs[n_in_big + 1:n_in_big + 1 + n_in_small]
    outs = list(refs[n_in_big + 1 + n_in_small:])
    out_big, out_small = outs[:n_out], outs[n_out:]
    if stacked:
        out_big[0] = out_big[0].at[0]
        out_small[0] = out_small[0].at[0]
    epilogue(jnp.dot(a_ref[...], w, preferred_element_type=F32), in_big, out_big)

    @pl.when(i == pl.num_programs(1) - 1)
    def _():
        epilogue(jnp.dot(a_small[...], w, preferred_element_type=F32), in_small, out_small)


def _matmul(big, small, w, layer, col0, ncols, epilogue, outs, name, tn=512):
    m, k = big.lhs.shape
    ms = small.lhs.shape[0]
    tm = min(m, 1024)
    tn = min(ncols, tn)
    assert m % tm == 0 and ncols % tn == 0 and col0 % tn == 0
    cb0 = col0 // tn

    in_specs = [pl.BlockSpec((tm, k), lambda j, i: (i, 0)),
                pl.BlockSpec((1, k, tn), lambda j, i: (layer, 0, cb0 + j))]
    args = [big.lhs, w]
    out_specs, out_shape, aliases = [], [], {}
    n_in = []
    for rows, nrow, tile in ((big, m, tm), (small, ms, ms)):
        rowmap = (lambda j, i: i) if rows is big else (lambda j, i: 0)
        n0 = len(args)
        if rows is small:
            in_specs.append(pl.BlockSpec((ms, k), lambda j, i: (0, 0)))
            args.append(small.lhs)
            n0 += 1
        for arr, kind in rows.extras:
            if kind == "mn":
                in_specs.append(pl.BlockSpec((tile, tn), lambda j, i, r=rowmap: (r(j, i), j)))
            else:
                in_specs.append(pl.BlockSpec((1, tn), lambda j, i: (0, j)))
            args.append(arr)
        specs = [pl.BlockSpec((tile, tn), lambda j, i, r=rowmap: (r(j, i), j)) for _ in outs]
        shapes = [jax.ShapeDtypeStruct((nrow, ncols), dt) for dt in outs]
        if rows.stack is not None:
            slot, buf = rows.stack
            assert buf.shape[1:] == (nrow, ncols) and buf.dtype == outs[0]
            specs[0] = pl.BlockSpec((1, tile, tn), lambda j, i, r=rowmap: (slot, r(j, i), j))
            shapes[0] = jax.ShapeDtypeStruct(buf.shape, outs[0])
            in_specs.append(pl.BlockSpec(memory_space=pl.ANY))
            args.append(buf)
            aliases[len(args) - 1] = len(out_specs)
        n_in.append(len(args) - n0)
        out_specs += specs
        out_shape += shapes
    assert (big.stack is None) == (small.stack is None)
    res = pl.pallas_call(
        functools.partial(_mm_kernel, epilogue, n_in[0], n_in[1], len(outs), big.stack is not None),
        grid=(ncols // tn, m // tm),
        in_specs=in_specs,
        out_specs=out_specs,
        out_shape=out_shape,
        scratch_shapes=[pltpu.VMEM((k, tn), BF16)],
        input_output_aliases=aliases,
        compiler_params=_cparams(("arbitrary", "arbitrary")),
        name=name,
    )(*args)
    return res[:len(outs)], res[len(outs):]


def _epi_f32_bf16(acc, extra, outs):
    outs[0][...] = acc
    outs[1][...] = acc.astype(BF16)


def _epi_f32(acc, extra, outs):
    outs[0][...] = acc


def _epi_silu(acc, extra, outs):
    outs[0][...] = acc * jax.nn.sigmoid(acc)


def _epi_scaled_bf16(acc, extra, outs):
    outs[0][...] = (acc * Q_SCALE).astype(BF16)


def _epi_residual(acc, extra, outs):
    outs[0][...] = extra[0][...] + acc


def _epi_glu(acc, extra, outs):
    g = extra[0][...]
    gate = jax.nn.sigmoid(acc + extra[2][...])
    outs[0][...] = (g * gate * extra[1][...]).astype(BF16)


def _s5_disc_kernel(lr_ref, li_ref, ls_ref, br_ref, bi_ref, ar_ref, ai_ref, bbr_ref, bbi_ref):
    step = jnp.exp(ls_ref[...])
    lr = lr_ref[...]
    li = li_ref[...]
    mag = jnp.exp(lr * step)
    ar = mag * jnp.cos(li * step)
    ai = mag * jnp.sin(li * step)
    den = lr * lr + li * li
    nr = ar - 1.0
    ni = ai
    fr = (nr * lr + ni * li) / den
    fi = (ni * lr - nr * li) / den
    br = br_ref[...]
    bi = bi_ref[...]
    ar_ref[...] = ar
    ai_ref[...] = ai
    bbr_ref[...] = fr * br - fi * bi
    bbi_ref[...] = fr * bi + fi * br


def _s5_pow_kernel(lr_ref, li_ref, st_ref, pr_ref, pi_ref):
    rows = pr_ref.shape[0]
    kk = (lax.broadcasted_iota(jnp.int32, (rows, 1), 0) + 1).astype(F32)
    st = jnp.exp(st_ref[...])
    ph_r = lr_ref[...] * st
    ph_i = li_ref[...] * st
    mag = jnp.exp(ph_r * kk)
    pr_ref[...] = mag * jnp.cos(ph_i * kk)
    pi_ref[...] = mag * jnp.sin(ph_i * kk)


def _s5_prepare(lam_re, lam_im, log_step, b_re, b_im, c_re, c_im, seg_len):
    g, p, c = S5_GROUPS, S5_STATE, S5_GROUP
    cp = c * p
    lr_b = jnp.broadcast_to(lam_re[:, None, :], (g, c, p)).reshape(g, cp)
    li_b = jnp.broadcast_to(lam_im[:, None, :], (g, c, p)).reshape(g, cp)
    br_t = jnp.swapaxes(b_re, 1, 2).reshape(g, cp)
    bi_t = jnp.swapaxes(b_im, 1, 2).reshape(g, cp)
    full = pl.BlockSpec((g, cp), lambda: (0, 0))
    ar_b, ai_b, bbr_t, bbi_t = pl.pallas_call(
        _s5_disc_kernel,
        in_specs=[full, full, pl.BlockSpec((g, 1), lambda: (0, 0)), full, full],
        out_specs=[full] * 4,
        out_shape=[jax.ShapeDtypeStruct((g, cp), F32)] * 4,
        name="s5_discretize",
    )(lr_b, li_b, log_step.reshape(g, 1), br_t, bi_t)
    a_re = ar_b[:, :p].reshape(1, g * p)
    a_im = ai_b[:, :p].reshape(1, g * p)

    nb, gl = N_GROUP_BLOCKS, GROUPS_PER_BLOCK
    eye = jnp.eye(gl, dtype=F32)
    def in_proj(bb):
        return jnp.einsum('bgcp,gh->bgchp', bb.reshape(nb, gl, c, p), eye).reshape(nb, gl * c, gl * p)
    wb = jnp.concatenate([in_proj(bbr_t), in_proj(bbi_t)], axis=-1).astype(BF16)
    def out_proj(cc):
        return jnp.einsum('bgcp,gh->bgphc', cc.reshape(nb, gl, c, p), eye).reshape(nb, gl * p, gl * c)
    wc = jnp.concatenate([out_proj(c_re), out_proj(-c_im)], axis=1).astype(BF16)

    n = g * p
    row = pl.BlockSpec((1, n), lambda: (0, 0))
    tab = pl.BlockSpec((seg_len, n), lambda: (0, 0))
    st_b = jnp.broadcast_to(log_step[:, None], (g, p)).reshape(1, n)
    p_re, p_im = pl.pallas_call(
        _s5_pow_kernel,
        in_specs=[row, row, row],
        out_specs=[tab, tab],
        out_shape=[jax.ShapeDtypeStruct((seg_len, n), F32)] * 2,
        name="s5_powers",
    )(lam_re.reshape(1, n), lam_im.reshape(1, n), st_b)
    return dict(a_re=a_re, a_im=a_im, wb=wb, wc=wc, p_re=p_re, p_im=p_im)


def _s5_scan_kernel(two_seg, seg_len, *refs):
    if two_seg:
        (u_ref, wb_ref, wc_ref, ar_ref, ai_ref, d_ref, pr_ref, pi_ref,
         g32_ref, g16_ref, hr_out, hi_out, up_ref, bu_ref, hb_ref, carry_ref) = refs
    else:
        (u_ref, wb_ref, wc_ref, ar_ref, ai_ref, d_ref, h0r_ref, h0i_ref,
         g32_ref, g16_ref, hr_out, hi_out, up_ref, bu_ref, hb_ref, carry_ref) = refs
    n_chain = SUBLANES
    rows = seg_len * n_chain
    ns = BLOCK_ST
    i = pl.program_id(1)
    last = pl.num_programs(1) - 1

    @pl.when(i == 0)
    def _():
        if two_seg:
            carry_ref[...] = jnp.zeros_like(carry_ref)
        else:
            carry_ref[:, :ns] = h0r_ref[...]
            carry_ref[:, ns:] = h0i_ref[...]

    for j in range(n_chain):
        if two_seg:
            b, s = j % 4, j // 4
            up_ref[:, j, :] = u_ref[b, s * seg_len:(s + 1) * seg_len, :]
        else:
            up_ref[:, j, :] = u_ref[j, :, :]
    upf = up_ref[...].reshape(rows, BLOCK_CH)
    bu_ref[...] = jnp.dot(upf.astype(BF16), wb_ref[0], preferred_element_type=F32)

    ar = jnp.broadcast_to(ar_ref[...], (n_chain, ns))
    ai = jnp.broadcast_to(ai_ref[...], (n_chain, ns))

    def step(k, carry):
        hr, hi = carry
        r = pl.multiple_of(k * n_chain, n_chain)
        nhr = ar * hr - ai * hi + bu_ref[pl.ds(r, n_chain), :ns]
        nhi = ar * hi + ai * hr + bu_ref[pl.ds(r, n_chain), ns:]
        bu_ref[pl.ds(r, n_chain), :ns] = nhr
        bu_ref[pl.ds(r, n_chain), ns:] = nhi
        return nhr, nhi

    hr, hi = lax.fori_loop(0, seg_len, step, (carry_ref[:, :ns], carry_ref[:, ns:]),
                           unroll=min(seg_len, 4))

    if two_seg:
        second_half = lax.broadcasted_iota(jnp.int32, (n_chain, ns), 0) >= n_chain // 2
        hm_r = jnp.where(second_half, pltpu.roll(hr, n_chain // 2, 0), 0.0)
        hm_i = jnp.where(second_half, pltpu.roll(hi, n_chain // 2, 0), 0.0)

        def power_row(k):
            def bcast(ref):
                return jnp.concatenate([jnp.broadcast_to(ref[k, cb:cb + 1, :], (n_chain, LANES))
                                        for cb in range(ns // LANES)], axis=1)
            return bcast(pr_ref), bcast(pi_ref)

        def fix(k2, _):
            r = pl.multiple_of(k2 * 2 * n_chain, 2 * n_chain)
            outs_r, outs_i = [], []
            for h in range(2):
                rr = r + h * n_chain
                pr, pi = power_row(k2 * 2 + h)
                outs_r.append(bu_ref[pl.ds(rr, n_chain), :ns] + (pr * hm_r - pi * hm_i))
                outs_i.append(bu_ref[pl.ds(rr, n_chain), ns:] + (pr * hm_i + pi * hm_r))
            hb_ref[pl.ds(r, 2 * n_chain), :ns] = jnp.concatenate(outs_r, axis=0).astype(BF16)
            hb_ref[pl.ds(r, 2 * n_chain), ns:] = jnp.concatenate(outs_i, axis=0).astype(BF16)
            return 0

        lax.fori_loop(0, seg_len // 2, fix, 0, unroll=2)
        pr, pi = power_row(seg_len - 1)
        end_r = hr + (pr * hm_r - pi * hm_i)
        end_i = hi + (pr * hm_i + pi * hm_r)
        first_half = jnp.logical_not(second_half)
        carry_ref[:, :ns] = jnp.where(first_half, pltpu.roll(end_r, n_chain // 2, 0), 0.0)
        carry_ref[:, ns:] = jnp.where(first_half, pltpu.roll(end_i, n_chain // 2, 0), 0.0)

        @pl.when(i == last)
        def _():
            hr_out[...] = end_r[n_chain // 2:, :]
            hi_out[...] = end_i[n_chain // 2:, :]
    else:
        hb_ref[...] = bu_ref[...].astype(BF16)
        carry_ref[:, :ns] = hr
        carry_ref[:, ns:] = hi

        @pl.when(i == last)
        def _():
            hr_out[...] = hr
            hi_out[...] = hi

    y = jnp.dot(hb_ref[...], wc_ref[0], preferred_element_type=F32)
    y = y + d_ref[...] * upf
    gl = jax.nn.gelu(y)
    up_ref[...] = gl.reshape(seg_len, n_chain, BLOCK_CH)
    for j in range(n_chain):
        v = up_ref[:, j, :]
        if two_seg:
            b, s = j % 4, j // 4
            g32_ref[b, s * seg_len:(s + 1) * seg_len, :] = v
        else:
            g32_ref[j, :, :] = v
    g16_ref[...] = g32_ref[...].astype(BF16)


def _s5_scan(u, prep, d, h0r, h0i, seg_len):
    bsz, seq, _ = u.shape
    two_seg = h0r is None
    n_st = S5_GROUPS * S5_STATE
    if two_seg:
        assert bsz == SUBLANES // 2 and seq % (2 * seg_len) == 0 and seg_len % 16 == 0
        t_tile = 2 * seg_len
    else:
        assert bsz == SUBLANES and seq == seg_len
        t_tile = seg_len
    nt = seq // t_tile
    rows = seg_len * SUBLANES
    ublock = pl.BlockSpec((bsz, t_tile, BLOCK_CH), lambda gb, i: (0, i, gb))
    strow = pl.BlockSpec((1, BLOCK_ST), lambda gb, i: (0, gb))
    in_specs = [ublock,
                pl.BlockSpec((1, BLOCK_CH, 2 * BLOCK_ST), lambda gb, i: (gb, 0, 0)),
                pl.BlockSpec((1, 2 * BLOCK_ST, BLOCK_CH), lambda gb, i: (gb, 0, 0)),
                strow, strow,
                pl.BlockSpec((1, BLOCK_CH), lambda gb, i: (0, gb))]
    args = [u, prep["wb"], prep["wc"], prep["a_re"], prep["a_im"], d]
    if two_seg:
        tab = pl.BlockSpec((seg_len, BLOCK_ST // LANES, LANES), lambda gb, i: (0, gb, 0))
        in_specs += [tab, tab]
        args += [prep["p_re"].reshape(seg_len, -1, LANES), prep["p_im"].reshape(seg_len, -1, LANES)]
    else:
        st = pl.BlockSpec((bsz, BLOCK_ST), lambda gb, i: (0, gb))
        in_specs += [st, st]
        args += [h0r, h0i]
    stout = pl.BlockSpec((bsz, BLOCK_ST), lambda gb, i: (0, gb))
    return pl.pallas_call(
        functools.partial(_s5_scan_kernel, two_seg, seg_len),
        grid=(N_GROUP_BLOCKS, nt),
        in_specs=in_specs,
        out_specs=[ublock, ublock, stout, stout],
        out_shape=[jax.ShapeDtypeStruct(u.shape, F32), jax.ShapeDtypeStruct(u.shape, BF16),
                   jax.ShapeDtypeStruct((bsz, n_st), F32), jax.ShapeDtypeStruct((bsz, n_st), F32)],
        scratch_shapes=[pltpu.VMEM((seg_len, SUBLANES, BLOCK_CH), F32),
                        pltpu.VMEM((rows, 2 * BLOCK_ST), F32),
                        pltpu.VMEM((rows, 2 * BLOCK_ST), BF16),
                        pltpu.VMEM((SUBLANES, 2 * BLOCK_ST), F32)],
        compiler_params=_cparams(("parallel", "arbitrary")),
        name="s5_scan_prompt" if two_seg else "s5_scan_sample",
    )(*args)


def _lambda_value(lq1, lk1, lq2, lk2, lambda_init):
    return (jnp.exp(jnp.sum(lq1 * lk1, axis=-1, keepdims=True))
            - jnp.exp(jnp.sum(lq2 * lk2, axis=-1, keepdims=True)) + lambda_init)


def _subln_gate(o, subln_g, lambda_init, sz):
    ms = jnp.mean(o * o, axis=-1, keepdims=True)
    return (o * lax.rsqrt(ms + EPS) * subln_g) * (1.0 - lambda_init) * sz


def _flash_kernel(lambda_init, tq, tk, q_ref, k_ref, v_ref, sz_ref, sg_ref,
                  lq1_ref, lk1_ref, lq2_ref, lk2_ref, o_ref, m_ref, l_ref, acc_ref):
    qi = pl.program_id(2)
    ratio = tq // tk
    first_diag = qi * ratio
    m_ref[...] = jnp.full_like(m_ref, NEG_BIG)
    l_ref[...] = jnp.zeros_like(l_ref)
    acc_ref[...] = jnp.zeros_like(acc_ref)

    def update(ki, masked):
        r = pl.multiple_of(ki * tk, tk)
        v = v_ref[pl.ds(r, tk), :]
        if masked:
            keep = (lax.broadcasted_iota(jnp.int32, (tq, tk), 1) + ki * tk
                    <= lax.broadcasted_iota(jnp.int32, (tq, tk), 0) + qi * tq)
        for c in range(2):
            q = q_ref[:, c * HEAD_DIM:(c + 1) * HEAD_DIM]
            k = k_ref[pl.ds(r, tk), c * HEAD_DIM:(c + 1) * HEAD_DIM]
            s = lax.dot_general(q, k, (((1,), (1,)), ((), ())), preferred_element_type=F32)
            if masked:
                s = jnp.where(keep, s, NEG_BIG)
            m_old = m_ref[c]
            m_new = jnp.maximum(m_old, jnp.max(s, axis=-1, keepdims=True))
            alpha = jnp.exp2(m_old - m_new)
            p = jnp.exp2(s - jnp.tile(m_new, (1, tk // LANES)))
            l_ref[c] = alpha * l_ref[c] + jnp.sum(p, axis=-1, keepdims=True)
            acc_ref[c] = (jnp.tile(alpha, (1, VALUE_DIM // LANES)) * acc_ref[c]
                          + jnp.dot(p.astype(BF16), v, preferred_element_type=F32))
            m_ref[c] = m_new

    def visible_block(ki, carry):
        update(ki, False)
        return carry

    lax.fori_loop(0, first_diag, visible_block, 0)
    for d in range(ratio):
        update(first_diag + d, True)

    lam = _lambda_value(lq1_ref[...], lk1_ref[...], lq2_ref[...], lk2_ref[...], lambda_init)
    rep = VALUE_DIM // LANES
    o = (acc_ref[0] / jnp.tile(l_ref[0], (1, rep))
         - lam * (acc_ref[1] / jnp.tile(l_ref[1], (1, rep))))
    o_ref[...] = _subln_gate(o, sg_ref[...], lambda_init, sz_ref[...]).astype(o_ref.dtype)


def _flash_attention(q, k, v, sz, subln_g, lam_vecs, lambda_init, bsz, seq):
    tq = min(seq, FLASH_TQ)
    tk = min(seq, FLASH_TK)
    assert seq % tq == 0 and tq % tk == 0
    nq = seq // tq
    qspec = pl.BlockSpec((tq, VALUE_DIM), lambda b, h, qi: (b * nq + qi, h))
    kspec = pl.BlockSpec((seq, VALUE_DIM), lambda b, h, qi: (b, h))
    vec = pl.BlockSpec((1, HEAD_DIM), lambda b, h, qi: (0, 0))
    return pl.pallas_call(
        functools.partial(_flash_kernel, lambda_init, tq, tk),
        grid=(bsz, N_HEADS, nq),
        in_specs=[qspec, kspec, kspec, qspec,
                  pl.BlockSpec((1, VALUE_DIM), lambda b, h, qi: (0, 0)),
                  vec, vec, vec, vec],
        out_specs=qspec,
        out_shape=jax.ShapeDtypeStruct((bsz * seq, D_MODEL), BF16),
        scratch_shapes=[pltpu.VMEM((2, tq, LANES), F32), pltpu.VMEM((2, tq, LANES), F32),
                        pltpu.VMEM((2, tq, VALUE_DIM), F32)],
        compiler_params=_cparams(("parallel", "parallel", "arbitrary")),
        name="flash_diff_attention",
    )(q, k, v, sz, subln_g.reshape(1, VALUE_DIM), *[x.reshape(1, HEAD_DIM) for x in lam_vecs])


HEADS_PER_STEP = SUBLANES
PAGES_PER_STEP = 4


def _decode_kernel(lambda_init, n_tok, pt_ref, qm_ref, *refs):
    pps = PAGES_PER_STEP
    kc = refs[:pps]
    vc = refs[pps:2 * pps]
    (kn_ref, vn_ref, sz_ref, sg_ref, lq1_ref, lk1_ref, lq2_ref, lk2_ref,
     o_ref, m_ref, l_ref, acc_ref, bias_ref) = refs[2 * pps:]
    step = pl.program_id(2)
    last = pl.num_programs(2) - 1
    rows_per_head = 2 * n_tok
    n_rows = HEADS_PER_STEP * rows_per_head
    page_rows = PAGE_SIZE * HEADS_PER_STEP

    @pl.when(step == 0)
    def _():
        m_ref[...] = jnp.full_like(m_ref, NEG_BIG)
        l_ref[...] = jnp.zeros_like(l_ref)
        acc_ref[...] = jnp.zeros_like(acc_ref)
        row_head = lax.broadcasted_iota(jnp.int32, bias_ref.shape, 0) // rows_per_head
        col_head = lax.broadcasted_iota(jnp.int32, bias_ref.shape, 1) % HEADS_PER_STEP
        bias_ref[...] = jnp.where(row_head == col_head, 0.0, NEG_BIG)

    def update(kblk, vblk, bias):
        s = lax.dot_general(qm_ref[0, 0], kblk, (((1,), (1,)), ((), ())), preferred_element_type=F32)
        s = s + bias
        m_old = m_ref[...]
        m_new = jnp.maximum(m_old, jnp.max(s, axis=-1, keepdims=True))
        alpha = jnp.exp2(m_old - m_new)
        p = jnp.exp2(s - jnp.tile(m_new, (1, s.shape[1] // LANES)))
        l_ref[...] = alpha * l_ref[...] + jnp.sum(p, axis=-1, keepdims=True)
        m_ref[...] = m_new
        acc_ref[...] = (jnp.tile(alpha, (1, VALUE_DIM // LANES)) * acc_ref[...]
                        + jnp.dot(p.astype(BF16), vblk, preferred_element_type=F32))

    kblk = jnp.concatenate([r[0].reshape(page_rows, VALUE_DIM).astype(BF16) for r in kc], axis=0)
    vblk = jnp.concatenate([r[0].reshape(page_rows, VALUE_DIM).astype(BF16) for r in vc], axis=0)
    update(kblk, vblk, bias_ref[...])

    @pl.when(step == last)
    def _():
        shape = (n_rows, LANES)
        row = lax.broadcasted_iota(jnp.int32, shape, 0)
        col = lax.broadcasted_iota(jnp.int32, shape, 1)
        keep = jnp.logical_and(row // rows_per_head == col % HEADS_PER_STEP,
                               col // HEADS_PER_STEP <= row % n_tok)
        update(kn_ref[0, 0], vn_ref[0, 0], jnp.where(keep, 0.0, NEG_BIG))
        lam = _lambda_value(lq1_ref[...], lk1_ref[...], lq2_ref[...], lk2_ref[...], lambda_init)
        on = acc_ref[...] / jnp.tile(l_ref[...], (1, VALUE_DIM // LANES))
        for h in range(HEADS_PER_STEP):
            r0 = h * rows_per_head
            o = on[r0:r0 + n_tok, :] - lam * on[r0 + n_tok:r0 + 2 * n_tok, :]
            cols = slice(h * VALUE_DIM, (h + 1) * VALUE_DIM)
            o_ref[0, :, cols] = _subln_gate(o, sg_ref[...], lambda_init, sz_ref[0, :, cols]).astype(o_ref.dtype)


def _decode_attention(q, k_new, v_new, sz, cache_k, cache_v, page_off, page_table, subln_g, lam_vecs,
                      lambda_init, bsz, n_tok):
    n_pages = page_table.shape[1]
    pps = PAGES_PER_STEP
    hps = HEADS_PER_STEP
    n_hb = N_HEADS // hps
    assert n_pages % pps == 0 and n_tok * hps <= LANES
    rows = hps * 2 * n_tok
    q6 = q.reshape(bsz, n_tok, n_hb, hps, 2, HEAD_DIM)
    eye = jnp.eye(2, dtype=BF16)
    qm = jnp.einsum('btnhcd,ck->bnhctkd', q6, eye).reshape(bsz, n_hb, rows, VALUE_DIM)

    def new_rows(x):
        x = x.reshape(bsz, n_tok, n_hb, hps, VALUE_DIM).transpose(0, 2, 1, 3, 4)
        x = x.reshape(bsz, n_hb, n_tok * hps, VALUE_DIM)
        return jnp.pad(x, ((0, 0), (0, 0), (0, LANES - n_tok * hps), (0, 0)))

    def page_spec(r):
        return pl.BlockSpec((1, PAGE_SIZE, hps, VALUE_DIM),
                            lambda b, hb, s, pt: (page_off + pt[b, s * pps + r], 0, hb, 0))
    newspec = pl.BlockSpec((1, 1, LANES, VALUE_DIM), lambda b, hb, s, pt: (b, hb, 0, 0))
    tokspec = pl.BlockSpec((1, n_tok, hps * VALUE_DIM), lambda b, hb, s, pt: (b, 0, hb))
    vec = pl.BlockSpec((1, HEAD_DIM), lambda b, hb, s, pt: (0, 0))
    grid_spec = pltpu.PrefetchScalarGridSpec(
        num_scalar_prefetch=1,
        grid=(bsz, n_hb, n_pages // pps),
        in_specs=[pl.BlockSpec((1, 1, rows, VALUE_DIM), lambda b, hb, s, pt: (b, hb, 0, 0))]
                 + [page_spec(r) for r in range(pps)] + [page_spec(r) for r in range(pps)]
                 + [newspec, newspec, tokspec,
                    pl.BlockSpec((1, VALUE_DIM), lambda b, hb, s, pt: (0, 0)), vec, vec, vec, vec],
        out_specs=tokspec,
        scratch_shapes=[pltpu.VMEM((rows, LANES), F32), pltpu.VMEM((rows, LANES), F32),
                        pltpu.VMEM((rows, VALUE_DIM), F32),
                        pltpu.VMEM((rows, pps * PAGE_SIZE * hps), F32)],
    )
    out = pl.pallas_call(
        functools.partial(_decode_kernel, lambda_init, n_tok),
        grid_spec=grid_spec,
        out_shape=jax.ShapeDtypeStruct((bsz, n_tok, D_MODEL), BF16),
        compiler_params=_cparams(("parallel", "parallel", "arbitrary")),
        name="decode_diff_attention",
    )(page_table, qm, *([cache_k] * pps), *([cache_v] * pps),
      new_rows(k_new), new_rows(v_new),
      sz.reshape(bsz, n_tok, D_MODEL), subln_g.reshape(1, VALUE_DIM),
      *[x.reshape(1, HEAD_DIM) for x in lam_vecs])
    return out.reshape(bsz * n_tok, D_MODEL)


def _lambda_init(layer):
    return 0.8 - 0.6 * math.exp(-0.3 * layer)


def _s5_layer(hs2, xn2, idx, w_in, w_glu, b_glu, w_out, prep, d, h0, dims, seg_len):
    (u_p,), (u_s,) = _matmul(_Rows(xn2[0]), _Rows(xn2[1]), w_in, idx, 0, D_MODEL, _epi_f32, [F32], "s5_in_u")
    (sz_p,), (sz_s,) = _matmul(_Rows(xn2[0]), _Rows(xn2[1]), w_in, idx, D_MODEL, D_MODEL, _epi_silu, [F32],
                               "s5_in_z")
    gated, states = [], []
    for u, sz, (bsz, seq), h0ri, sl in ((u_p, sz_p, dims[0], (None, None), seg_len),
                                        (u_s, sz_s, dims[1], h0, dims[1][1])):
        g32, g16, hr, hi = _s5_scan(u.reshape(bsz, seq, D_MODEL), prep, d, h0ri[0], h0ri[1], sl)
        gated.append(_Rows(g16.reshape(bsz * seq, D_MODEL),
                           [(g32.reshape(bsz * seq, D_MODEL), "mn"), (sz, "mn"), (b_glu.reshape(1, D_MODEL), "n")]))
        shape = (bsz, S5_GROUPS, S5_STATE)
        states.append((hr.reshape(shape), hi.reshape(shape)))
    (m_p,), (m_s,) = _matmul(gated[0], gated[1], w_glu, idx, 0, D_MODEL, _epi_glu, [BF16], "s5_glu")
    (h_p,), (h_s,) = _matmul(_Rows(m_p, [(hs2[0], "mn")]), _Rows(m_s, [(hs2[1], "mn")]), w_out, idx, 0, D_MODEL,
                             _epi_residual, [F32], "s5_out")
    return (h_p, h_s), states


def _attn_in_proj(xn2, w_in, idx, k_stacks, v_stacks):
    big, small = _Rows(xn2[0]), _Rows(xn2[1])
    q = _matmul(big, small, w_in, idx, 0, D_MODEL, _epi_scaled_bf16, [BF16], "attn_in_q")
    k = _matmul(_Rows(xn2[0], stack=(idx, k_stacks[0])), _Rows(xn2[1], stack=(idx, k_stacks[1])), w_in, idx,
                D_MODEL, D_MODEL, _epi_f32_bf16, [F32, BF16], "attn_in_k")
    v = _matmul(_Rows(xn2[0], stack=(idx, v_stacks[0])), _Rows(xn2[1], stack=(idx, v_stacks[1])), w_in, idx,
                2 * D_MODEL, D_MODEL, _epi_f32_bf16, [F32, BF16], "attn_in_v")
    sz = _matmul(big, small, w_in, idx, 3 * D_MODEL, D_MODEL, _epi_silu, [F32], "attn_in_z")
    return q, k, v, sz


def kernel(x_prompt, x_sample, cache_k, cache_v, state_s5_re, state_s5_im, page_table, norm_g, final_norm_g, s5_w_in, s5_lambda_re, s5_lambda_im, s5_log_step, s5_b_re, s5_b_im, s5_c_re, s5_c_im, s5_d, s5_w_glu, s5_b_glu, s5_w_out, attn_w_in, attn_lam_q1, attn_lam_k1, attn_lam_q2, attn_lam_k2, attn_subln_g, attn_w_out):
    bp, lp, _ = x_prompt.shape
    bs, ls, _ = x_sample.shape
    depth = norm_g.shape[0]
    n_pool = cache_k.shape[1]
    seg_len = min(128, lp // 2)

    hp = x_prompt.reshape(bp * lp, D_MODEL)
    hs = x_sample.reshape(bs * ls, D_MODEL)
    n_attn = depth // 2
    k_p = jnp.zeros((n_attn, bp * lp, D_MODEL), F32)
    v_p = jnp.zeros((n_attn, bp * lp, D_MODEL), F32)
    k_s = jnp.zeros((n_attn, bs * ls, D_MODEL), F32)
    v_s = jnp.zeros((n_attn, bs * ls, D_MODEL), F32)
    sr_p, si_p, sr_s, si_s = [], [], [], []
    for layer in range(depth):
        idx = layer // 2
        xn2 = (_rmsnorm(hp, norm_g[layer], BF16), _rmsnorm(hs, norm_g[layer], BF16))
        if layer % 2 == 0:
            prep = _s5_prepare(s5_lambda_re[idx], s5_lambda_im[idx], s5_log_step[idx],
                               s5_b_re[idx], s5_b_im[idx], s5_c_re[idx], s5_c_im[idx], seg_len)
            d = s5_d[idx].reshape(1, D_MODEL)
            h0 = (state_s5_re[idx].reshape(bs, S5_GROUPS * S5_STATE),
                  state_s5_im[idx].reshape(bs, S5_GROUPS * S5_STATE))
            (hp, hs), states = _s5_layer((hp, hs), xn2, idx, s5_w_in, s5_w_glu, s5_b_glu[idx], s5_w_out,
                                         prep, d, h0, ((bp, lp), (bs, ls)), seg_len)
            sr_p.append(states[0][0]); si_p.append(states[0][1])
            sr_s.append(states[1][0]); si_s.append(states[1][1])
        else:
            lam0 = _lambda_init(layer)
            lam_vecs = (attn_lam_q1[idx], attn_lam_k1[idx], attn_lam_q2[idx], attn_lam_k2[idx])
            q, k, v, sz = _attn_in_proj(xn2, attn_w_in, idx, (k_p, k_s), (v_p, v_s))
            (k_p, k16_p), (k_s, k16_s) = k
            (v_p, v16_p), (v_s, v16_s) = v
            m_p = _flash_attention(q[0][0], k16_p, v16_p, sz[0][0], attn_subln_g[idx], lam_vecs, lam0, bp, lp)
            m_s = _decode_attention(q[1][0], k16_s, v16_s, sz[1][0],
                                    cache_k.reshape(-1, PAGE_SIZE, N_HEADS, VALUE_DIM),
                                    cache_v.reshape(-1, PAGE_SIZE, N_HEADS, VALUE_DIM),
                                    idx * n_pool, page_table, attn_subln_g[idx], lam_vecs, lam0, bs, ls)
            (hp,), (hs,) = _matmul(_Rows(m_p, [(hp, "mn")]), _Rows(m_s, [(hs, "mn")]), attn_w_out, idx, 0, D_MODEL,
                                   _epi_residual, [F32], "attn_out")
    y_prompt = _rmsnorm(hp, final_norm_g, F32).reshape(bp, lp, D_MODEL)
    y_sample = _rmsnorm(hs, final_norm_g, F32).reshape(bs, ls, D_MODEL)
    kv_p = (n_attn, bp, lp, N_HEADS, VALUE_DIM)
    kv_s = (n_attn, bs, ls, N_HEADS, VALUE_DIM)
    return (y_prompt, y_sample, k_p.reshape(kv_p), v_p.reshape(kv_p), k_s.reshape(kv_s), v_s.reshape(kv_s),
            jnp.stack(sr_p), jnp.stack(si_p), jnp.stack(sr_s), jnp.stack(si_s))
```

```python
import functools
import math

import jax
import jax.numpy as jnp
from jax import lax
from jax.experimental import pallas as pl
from jax.experimental.pallas import tpu as pltpu

F32 = jnp.float32
BF16 = jnp.bfloat16

D_MODEL = 4096
S5_GROUP = 16
S5_STATE = 64
S5_GROUPS = D_MODEL // S5_GROUP
GROUPS_PER_BLOCK = 16
N_GROUP_BLOCKS = S5_GROUPS // GROUPS_PER_BLOCK
BLOCK_CH = GROUPS_PER_BLOCK * S5_GROUP
BLOCK_ST = GROUPS_PER_BLOCK * S5_STATE
HEAD_DIM = 128
VALUE_DIM = 2 * HEAD_DIM
N_HEADS = D_MODEL // VALUE_DIM
PAGE_SIZE = 128
EPS = 1e-6
ATTN_SCALE = 1.0 / math.sqrt(HEAD_DIM)
Q_SCALE = ATTN_SCALE * math.log2(math.e)
NEG_BIG = -1e30
SUBLANES = 8
LANES = 128
FLASH_TQ = 1024
FLASH_TK = 256
VMEM_LIMIT = 56 * 1024 * 1024


def _cparams(sem):
    return pltpu.CompilerParams(dimension_semantics=sem, vmem_limit_bytes=VMEM_LIMIT)


def _rmsnorm_kernel(x_ref, g_ref, o_ref):
    x = x_ref[...]
    ms = jnp.mean(x * x, axis=-1, keepdims=True)
    o_ref[...] = (x * lax.rsqrt(ms + EPS) * g_ref[...]).astype(o_ref.dtype)


def _rmsnorm(x, g, out_dtype):
    m, d = x.shape
    tm = min(m, 256)
    return pl.pallas_call(
        _rmsnorm_kernel,
        grid=(m // tm,),
        in_specs=[pl.BlockSpec((tm, d), lambda i: (i, 0)),
                  pl.BlockSpec((1, d), lambda i: (0, 0))],
        out_specs=pl.BlockSpec((tm, d), lambda i: (i, 0)),
        out_shape=jax.ShapeDtypeStruct((m, d), out_dtype),
        compiler_params=_cparams(("parallel",)),
        name="rmsnorm",
    )(x, g.reshape(1, d))


class _Rows:
    def __init__(self, lhs, extras=(), stack=None):
        self.lhs, self.extras, self.stack = lhs, list(extras), stack


def _mm_kernel(epilogue, n_in_big, n_in_small, n_out, stacked, a_ref, w_ref, *refs):
    wbf_ref = refs[-1]
    refs = refs[:-1]
    i = pl.program_id(1)

    @pl.when(i == 0)
    def _():
        wbf_ref[...] = w_ref[0].astype(BF16)

    w = wbf_ref[...]
    in_big = refs[:n_in_big]
    a_small = refs[n_in_big]
    in_small = refs[n_in_big + 1:n_in_big + 1 + n_in_small]
    outs = list(refs[n_in_big + 1 + n_in_small:])
    out_big, out_small = outs[:n_out], outs[n_out:]
    if stacked:
        out_big[0] = out_big[0].at[0]
        out_small[0] = out_small[0].at[0]
    epilogue(jnp.dot(a_ref[...], w, preferred_element_type=F32), in_big, out_big)

    @pl.when(i == pl.num_programs(1) - 1)
    def _():
        epilogue(jnp.dot(a_small[...], w, preferred_element_type=F32), in_small, out_small)


def _matmul(big, small, w, layer, col0, ncols, epilogue, outs, name, tn=512):
    m, k = big.lhs.shape
    ms = small.lhs.shape[0]
    tm = min(m, 1024)
    tn = min(ncols, tn)
    assert m % tm == 0 and ncols % tn == 0 and col0 % tn == 0
    cb0 = col0 // tn

    in_specs = [pl.BlockSpec((tm, k), lambda j, i: (i, 0)),
                pl.BlockSpec((1, k, tn), lambda j, i: (layer, 0, cb0 + j))]
    args = [big.lhs, w]
    out_specs, out_shape, aliases = [], [], {}
    n_in = []
    for rows, nrow, tile in ((big, m, tm), (small, ms, ms)):
        rowmap = (lambda j, i: i) if rows is big else (lambda j, i: 0)
        n0 = len(args)
        if rows is small:
            in_specs.append(pl.BlockSpec((ms, k), lambda j, i: (0, 0)))
            args.append(small.lhs)
            n0 += 1
        for arr, kind in rows.extras:
            if kind == "mn":
                in_specs.append(pl.BlockSpec((tile, tn), lambda j, i, r=rowmap: (r(j, i), j)))
            else:
                in_specs.append(pl.BlockSpec((1, tn), lambda j, i: (0, j)))
            args.append(arr)
        specs = [pl.BlockSpec((tile, tn), lambda j, i, r=rowmap: (r(j, i), j)) for _ in outs]
        shapes = [jax.ShapeDtypeStruct((nrow, ncols), dt) for dt in outs]
        if rows.stack is not None:
            slot, buf = rows.stack
            assert buf.shape[1:] == (nrow, ncols) and buf.dtype == outs[0]
            specs[0] = pl.BlockSpec((1, tile, tn), lambda j, i, r=rowmap: (slot, r(j, i), j))
            shapes[0] = jax.ShapeDtypeStruct(buf.shape, outs[0])
            in_specs.append(pl.BlockSpec(memory_space=pl.ANY))
            args.append(buf)
            aliases[len(args) - 1] = len(out_specs)
        n_in.append(len(args) - n0)
        out_specs += specs
        out_shape += shapes
    assert (big.stack is None) == (small.stack is None)
    res = pl.pallas_call(
        functools.partial(_mm_kernel, epilogue, n_in[0], n_in[1], len(outs), big.stack is not None),
        grid=(ncols // tn, m // tm),
        in_specs=in_specs,
        out_specs=out_specs,
        out_shape=out_shape,
        scratch_shapes=[pltpu.VMEM((k, tn), BF16)],
        input_output_aliases=aliases,
        compiler_params=_cparams(("arbitrary", "arbitrary")),
        name=name,
    )(*args)
    return res[:len(outs)], res[len(outs):]


def _epi_f32_bf16(acc, extra, outs):
    outs[0][...] = acc
    outs[1][...] = acc.astype(BF16)


def _epi_f32(acc, extra, outs):
    outs[0][...] = acc


def _epi_silu(acc, extra, outs):
    outs[0][...] = acc * jax.nn.sigmoid(acc)


def _epi_scaled_bf16(acc, extra, outs):
    outs[0][...] = (acc * Q_SCALE).astype(BF16)


def _epi_residual(acc, extra, outs):
    outs[0][...] = extra[0][...] + acc


def _epi_glu(acc, extra, outs):
    g = extra[0][...]
    gate = jax.nn.sigmoid(acc + extra[2][...])
    outs[0][...] = (g * gate * extra[1][...]).astype(BF16)


def _s5_disc_kernel(lr_ref, li_ref, ls_ref, br_ref, bi_ref, ar_ref, ai_ref, bbr_ref, bbi_ref):
    step = jnp.exp(ls_ref[...])
    lr = lr_ref[...]
    li = li_ref[...]
    mag = jnp.exp(lr * step)
    ar = mag * jnp.cos(li * step)
    ai = mag * jnp.sin(li * step)
    den = lr * lr + li * li
    nr = ar - 1.0
    ni = ai
    fr = (nr * lr + ni * li) / den
    fi = (ni * lr - nr * li) / den
    br = br_ref[...]
    bi = bi_ref[...]
    ar_ref[...] = ar
    ai_ref[...] = ai
    bbr_ref[...] = fr * br - fi * bi
    bbi_ref[...] = fr * bi + fi * br


def _s5_pow_kernel(lr_ref, li_ref, st_ref, pr_ref, pi_ref):
    rows = pr_ref.shape[0]
    kk = (lax.broadcasted_iota(jnp.int32, (rows, 1), 0) + 1).astype(F32)
    st = jnp.exp(st_ref[...])
    ph_r = lr_ref[...] * st
    ph_i = li_ref[...] * st
    mag = jnp.exp(ph_r * kk)
    pr_ref[...] = mag * jnp.cos(ph_i * kk)
    pi_ref[...] = mag * jnp.sin(ph_i * kk)


def _s5_prepare(lam_re, lam_im, log_step, b_re, b_im, c_re, c_im, seg_len):
    g, p, c = S5_GROUPS, S5_STATE, S5_GROUP
    cp = c * p
    lr_b = jnp.broadcast_to(lam_re[:, None, :], (g, c, p)).reshape(g, cp)
    li_b = jnp.broadcast_to(lam_im[:, None, :], (g, c, p)).reshape(g, cp)
    br_t = jnp.swapaxes(b_re, 1, 2).reshape(g, cp)
    bi_t = jnp.swapaxes(b_im, 1, 2).reshape(g, cp)
    full = pl.BlockSpec((g, cp), lambda: (0, 0))
    ar_b, ai_b, bbr_t, bbi_t = pl.pallas_call(
        _s5_disc_kernel,
        in_specs=[full, full, pl.BlockSpec((g, 1), lambda: (0, 0)), full, full],
        out_specs=[full] * 4,
        out_shape=[jax.ShapeDtypeStruct((g, cp), F32)] * 4,
        name="s5_discretize",
    )(lr_b, li_b, log_step.reshape(g, 1), br_t, bi_t)
    a_re = ar_b[:, :p].reshape(1, g * p)
    a_im = ai_b[:, :p].reshape(1, g * p)

    nb, gl = N_GROUP_BLOCKS, GROUPS_PER_BLOCK
    eye = jnp.eye(gl, dtype=F32)
    def in_proj(bb):
        return jnp.einsum('bgcp,gh->bgchp', bb.reshape(nb, gl, c, p), eye).reshape(nb, gl * c, gl * p)
    wb = jnp.concatenate([in_proj(bbr_t), in_proj(bbi_t)], axis=-1).astype(BF16)
    def out_proj(cc):
        return jnp.einsum('bgcp,gh->bgphc', cc.reshape(nb, gl, c, p), eye).reshape(nb, gl * p, gl * c)
    wc = jnp.concatenate([out_proj(c_re), out_proj(-c_im)], axis=1).astype(BF16)

    n = g * p
    row = pl.BlockSpec((1, n), lambda: (0, 0))
    tab = pl.BlockSpec((seg_len, n), lambda: (0, 0))
    st_b = jnp.broadcast_to(log_step[:, None], (g, p)).reshape(1, n)
    p_re, p_im = pl.pallas_call(
        _s5_pow_kernel,
        in_specs=[row, row, row],
        out_specs=[tab, tab],
        out_shape=[jax.ShapeDtypeStruct((seg_len, n), F32)] * 2,
        name="s5_powers",
    )(lam_re.reshape(1, n), lam_im.reshape(1, n), st_b)
    return dict(a_re=a_re, a_im=a_im, wb=wb, wc=wc, p_re=p_re, p_im=p_im)


def _s5_scan_kernel(two_seg, seg_len, *refs):
    if two_seg:
        (u_ref, wb_ref, wc_ref, ar_ref, ai_ref, d_ref, pr_ref, pi_ref,
         g32_ref, g16_ref, hr_out, hi_out, up_ref, bu_ref, hb_ref, carry_ref) = refs
    else:
        (u_ref, wb_ref, wc_ref, ar_ref, ai_ref, d_ref, h0r_ref, h0i_ref,
         g32_ref, g16_ref, hr_out, hi_out, up_ref, bu_ref, hb_ref, carry_ref) = refs
    n_chain = SUBLANES
    rows = seg_len * n_chain
    ns = BLOCK_ST
    i = pl.program_id(1)
    last = pl.num_programs(1) - 1

    @pl.when(i == 0)
    def _():
        if two_seg:
            carry_ref[...] = jnp.zeros_like(carry_ref)
        else:
            carry_ref[:, :ns] = h0r_ref[...]
            carry_ref[:, ns:] = h0i_ref[...]

    for j in range(n_chain):
        if two_seg:
            b, s = j % 4, j // 4
            up_ref[:, j, :] = u_ref[b, s * seg_len:(s + 1) * seg_len, :]
        else:
            up_ref[:, j, :] = u_ref[j, :, :]
    upf = up_ref[...].reshape(rows, BLOCK_CH)
    bu_ref[...] = jnp.dot(upf.astype(BF16), wb_ref[0], preferred_element_type=F32)

    ar = jnp.broadcast_to(ar_ref[...], (n_chain, ns))
    ai = jnp.broadcast_to(ai_ref[...], (n_chain, ns))

    def step(k, carry):
        hr, hi = carry
        r = pl.multiple_of(k * n_chain, n_chain)
        nhr = ar * hr - ai * hi + bu_ref[pl.ds(r, n_chain), :ns]
        nhi = ar * hi + ai * hr + bu_ref[pl.ds(r, n_chain), ns:]
        bu_ref[pl.ds(r, n_chain), :ns] = nhr
        bu_ref[pl.ds(r, n_chain), ns:] = nhi
        return nhr, nhi

    hr, hi = lax.fori_loop(0, seg_len, step, (carry_ref[:, :ns], carry_ref[:, ns:]),
                           unroll=min(seg_len, 8))

    if two_seg:
        second_half = lax.broadcasted_iota(jnp.int32, (n_chain, ns), 0) >= n_chain // 2
        hm_r = jnp.where(second_half, pltpu.roll(hr, n_chain // 2, 0), 0.0)
        hm_i = jnp.where(second_half, pltpu.roll(hi, n_chain // 2, 0), 0.0)

        def power_row(k):
            def bcast(ref):
                return jnp.concatenate([jnp.broadcast_to(ref[k, cb:cb + 1, :], (n_chain, LANES))
                                        for cb in range(ns // LANES)], axis=1)
            return bcast(pr_ref), bcast(pi_ref)

        def fix(k2, _):
            r = pl.multiple_of(k2 * 2 * n_chain, 2 * n_chain)
            outs_r, outs_i = [], []
            for h in range(2):
                rr = r + h * n_chain
                pr, pi = power_row(k2 * 2 + h)
                outs_r.append(bu_ref[pl.ds(rr, n_chain), :ns] + (pr * hm_r - pi * hm_i))
                outs_i.append(bu_ref[pl.ds(rr, n_chain), ns:] + (pr * hm_i + pi * hm_r))
            hb_ref[pl.ds(r, 2 * n_chain), :ns] = jnp.concatenate(outs_r, axis=0).astype(BF16)
            hb_ref[pl.ds(r, 2 * n_chain), ns:] = jnp.concatenate(outs_i, axis=0).astype(BF16)
            return 0

        lax.fori_loop(0, seg_len // 2, fix, 0, unroll=2)
        pr, pi = power_row(seg_len - 1)
        end_r = hr + (pr * hm_r - pi * hm_i)
        end_i = hi + (pr * hm_i + pi * hm_r)
        first_half = jnp.logical_not(second_half)
        carry_ref[:, :ns] = jnp.where(first_half, pltpu.roll(end_r, n_chain // 2, 0), 0.0)
        carry_ref[:, ns:] = jnp.where(first_half, pltpu.roll(end_i, n_chain // 2, 0), 0.0)

        @pl.when(i == last)
        def _():
            hr_out[...] = end_r[n_chain // 2:, :]
            hi_out[...] = end_i[n_chain // 2:, :]
    else:
        hb_ref[...] = bu_ref[...].astype(BF16)
        carry_ref[:, :ns] = hr
        carry_ref[:, ns:] = hi

        @pl.when(i == last)
        def _():
            hr_out[...] = hr
            hi_out[...] = hi

    y = jnp.dot(hb_ref[...], wc_ref[0], preferred_element_type=F32)
    y = y + d_ref[...] * upf
    gl = jax.nn.gelu(y)
    up_ref[...] = gl.reshape(seg_len, n_chain, BLOCK_CH)
    for j in range(n_chain):
        v = up_ref[:, j, :]
        if two_seg:
            b, s = j % 4, j // 4
            g32_ref[b, s * seg_len:(s + 1) * seg_len, :] = v
        else:
            g32_ref[j, :, :] = v
    g16_ref[...] = g32_ref[...].astype(BF16)


def _s5_scan(u, prep, d, h0r, h0i, seg_len):
    bsz, seq, _ = u.shape
    two_seg = h0r is None
    n_st = S5_GROUPS * S5_STATE
    if two_seg:
        assert bsz == SUBLANES // 2 and seq % (2 * seg_len) == 0 and seg_len % 16 == 0
        t_tile = 2 * seg_len
    else:
        assert bsz == SUBLANES and seq == seg_len
        t_tile = seg_len
    nt = seq // t_tile
    rows = seg_len * SUBLANES
    ublock = pl.BlockSpec((bsz, t_tile, BLOCK_CH), lambda gb, i: (0, i, gb))
    strow = pl.BlockSpec((1, BLOCK_ST), lambda gb, i: (0, gb))
    in_specs = [ublock,
                pl.BlockSpec((1, BLOCK_CH, 2 * BLOCK_ST), lambda gb, i: (gb, 0, 0)),
                pl.BlockSpec((1, 2 * BLOCK_ST, BLOCK_CH), lambda gb, i: (gb, 0, 0)),
                strow, strow,
                pl.BlockSpec((1, BLOCK_CH), lambda gb, i: (0, gb))]
    args = [u, prep["wb"], prep["wc"], prep["a_re"], prep["a_im"], d]
    if two_seg:
        tab = pl.BlockSpec((seg_len, BLOCK_ST // LANES, LANES), lambda gb, i: (0, gb, 0))
        in_specs += [tab, tab]
        args += [prep["p_re"].reshape(seg_len, -1, LANES), prep["p_im"].reshape(seg_len, -1, LANES)]
    else:
        st = pl.BlockSpec((bsz, BLOCK_ST), lambda gb, i: (0, gb))
        in_specs += [st, st]
        args += [h0r, h0i]
    stout = pl.BlockSpec((bsz, BLOCK_ST), lambda gb, i: (0, gb))
    return pl.pallas_call(
        functools.partial(_s5_scan_kernel, two_seg, seg_len),
        grid=(N_GROUP_BLOCKS, nt),
        in_specs=in_specs,
        out_specs=[ublock, ublock, stout, stout],
        out_shape=[jax.ShapeDtypeStruct(u.shape, F32), jax.ShapeDtypeStruct(u.shape, BF16),
                   jax.ShapeDtypeStruct((bsz, n_st), F32), jax.ShapeDtypeStruct((bsz, n_st), F32)],
        scratch_shapes=[pltpu.VMEM((seg_len, SUBLANES, BLOCK_CH), F32),
                        pltpu.VMEM((rows, 2 * BLOCK_ST), F32),
                        pltpu.VMEM((rows, 2 * BLOCK_ST), BF16),
                        pltpu.VMEM((SUBLANES, 2 * BLOCK_ST), F32)],
        compiler_params=_cparams(("parallel", "arbitrary")),
        name="s5_scan_prompt" if two_seg else "s5_scan_sample",
    )(*args)


def _lambda_value(lq1, lk1, lq2, lk2, lambda_init):
    return (jnp.exp(jnp.sum(lq1 * lk1, axis=-1, keepdims=True))
            - jnp.exp(jnp.sum(lq2 * lk2, axis=-1, keepdims=True)) + lambda_init)


def _subln_gate(o, subln_g, lambda_init, sz):
    ms = jnp.mean(o * o, axis=-1, keepdims=True)
    return (o * lax.rsqrt(ms + EPS) * subln_g) * (1.0 - lambda_init) * sz


def _flash_kernel(lambda_init, tq, tk, q_ref, k_ref, v_ref, sz_ref, sg_ref,
                  lq1_ref, lk1_ref, lq2_ref, lk2_ref, o_ref, m_ref, l_ref, acc_ref):
    qi = pl.program_id(2)
    ratio = tq // tk
    first_diag = qi * ratio
    m_ref[...] = jnp.full_like(m_ref, NEG_BIG)
    l_ref[...] = jnp.zeros_like(l_ref)
    acc_ref[...] = jnp.zeros_like(acc_ref)

    def update(ki, masked):
        r = pl.multiple_of(ki * tk, tk)
        v = v_ref[pl.ds(r, tk), :]
        if masked:
            keep = (lax.broadcasted_iota(jnp.int32, (tq, tk), 1) + ki * tk
                    <= lax.broadcasted_iota(jnp.int32, (tq, tk), 0) + qi * tq)
        for c in range(2):
            q = q_ref[:, c * HEAD_DIM:(c + 1) * HEAD_DIM]
            k = k_ref[pl.ds(r, tk), c * HEAD_DIM:(c + 1) * HEAD_DIM]
            s = lax.dot_general(q, k, (((1,), (1,)), ((), ())), preferred_element_type=F32)
            if masked:
                s = jnp.where(keep, s, NEG_BIG)
            m_old = m_ref[c]
            m_new = jnp.maximum(m_old, jnp.max(s, axis=-1, keepdims=True))
            alpha = jnp.exp2(m_old - m_new)
            p = jnp.exp2(s - jnp.tile(m_new, (1, tk // LANES)))
            l_ref[c] = alpha * l_ref[c] + jnp.sum(p, axis=-1, keepdims=True)
            acc_ref[c] = (jnp.tile(alpha, (1, VALUE_DIM // LANES)) * acc_ref[c]
                          + jnp.dot(p.astype(BF16), v, preferred_element_type=F32))
            m_ref[c] = m_new

    def visible_block(ki, carry):
        update(ki, False)
        return carry

    lax.fori_loop(0, first_diag, visible_block, 0)
    for d in range(ratio):
        update(first_diag + d, True)

    lam = _lambda_value(lq1_ref[...], lk1_ref[...], lq2_ref[...], lk2_ref[...], lambda_init)
    rep = VALUE_DIM // LANES
    o = (acc_ref[0] / jnp.tile(l_ref[0], (1, rep))
         - lam * (acc_ref[1] / jnp.tile(l_ref[1], (1, rep))))
    o_ref[...] = _subln_gate(o, sg_ref[...], lambda_init, sz_ref[...]).astype(o_ref.dtype)


def _flash_attention(q, k, v, sz, subln_g, lam_vecs, lambda_init, bsz, seq):
    tq = min(seq, FLASH_TQ)
    tk = min(seq, FLASH_TK)
    assert seq % tq == 0 and tq % tk == 0
    nq = seq // tq
    qspec = pl.BlockSpec((tq, VALUE_DIM), lambda b, h, qi: (b * nq + qi, h))
    kspec = pl.BlockSpec((seq, VALUE_DIM), lambda b, h, qi: (b, h))
    vec = pl.BlockSpec((1, HEAD_DIM), lambda b, h, qi: (0, 0))
    return pl.pallas_call(
        functools.partial(_flash_kernel, lambda_init, tq, tk),
        grid=(bsz, N_HEADS, nq),
        in_specs=[qspec, kspec, kspec, qspec,
                  pl.BlockSpec((1, VALUE_DIM), lambda b, h, qi: (0, 0)),
                  vec, vec, vec, vec],
        out_specs=qspec,
        out_shape=jax.ShapeDtypeStruct((bsz * seq, D_MODEL), BF16),
        scratch_shapes=[pltpu.VMEM((2, tq, LANES), F32), pltpu.VMEM((2, tq, LANES), F32),
                        pltpu.VMEM((2, tq, VALUE_DIM), F32)],
        compiler_params=_cparams(("parallel", "parallel", "arbitrary")),
        name="flash_diff_attention",
    )(q, k, v, sz, subln_g.reshape(1, VALUE_DIM), *[x.reshape(1, HEAD_DIM) for x in lam_vecs])


HEADS_PER_STEP = SUBLANES
PAGES_PER_STEP = 8


def _decode_kernel(lambda_init, n_tok, pt_ref, qm_ref, *refs):
    pps = PAGES_PER_STEP
    kc = refs[:pps]
    vc = refs[pps:2 * pps]
    (kn_ref, vn_ref, sz_ref, sg_ref, lq1_ref, lk1_ref, lq2_ref, lk2_ref,
     o_ref, m_ref, l_ref, acc_ref, bias_ref) = refs[2 * pps:]
    step = pl.program_id(2)
    last = pl.num_programs(2) - 1
    rows_per_head = 2 * n_tok
    n_rows = HEADS_PER_STEP * rows_per_head
    page_rows = PAGE_SIZE * HEADS_PER_STEP

    @pl.when(step == 0)
    def _():
        m_ref[...] = jnp.full_like(m_ref, NEG_BIG)
        l_ref[...] = jnp.zeros_like(l_ref)
        acc_ref[...] = jnp.zeros_like(acc_ref)
        row_head = lax.broadcasted_iota(jnp.int32, bias_ref.shape, 0) // rows_per_head
        col_head = lax.broadcasted_iota(jnp.int32, bias_ref.shape, 1) % HEADS_PER_STEP
        bias_ref[...] = jnp.where(row_head == col_head, 0.0, NEG_BIG)

    def update(kblk, vblk, bias):
        s = lax.dot_general(qm_ref[0, 0], kblk, (((1,), (1,)), ((), ())), preferred_element_type=F32)
        s = s + bias
        m_old = m_ref[...]
        m_new = jnp.maximum(m_old, jnp.max(s, axis=-1, keepdims=True))
        alpha = jnp.exp2(m_old - m_new)
        p = jnp.exp2(s - jnp.tile(m_new, (1, s.shape[1] // LANES)))
        l_ref[...] = alpha * l_ref[...] + jnp.sum(p, axis=-1, keepdims=True)
        m_ref[...] = m_new
        acc_ref[...] = (jnp.tile(alpha, (1, VALUE_DIM // LANES)) * acc_ref[...]
                        + jnp.dot(p.astype(BF16), vblk, preferred_element_type=F32))

    kblk = jnp.concatenate([r[0].reshape(page_rows, VALUE_DIM).astype(BF16) for r in kc], axis=0)
    vblk = jnp.concatenate([r[0].reshape(page_rows, VALUE_DIM).astype(BF16) for r in vc], axis=0)
    update(kblk, vblk, bias_ref[...])

    @pl.when(step == last)
    def _():
        shape = (n_rows, LANES)
        row = lax.broadcasted_iota(jnp.int32, shape, 0)
        col = lax.broadcasted_iota(jnp.int32, shape, 1)
        keep = jnp.logical_and(row // rows_per_head == col % HEADS_PER_STEP,
                               col // HEADS_PER_STEP <= row % n_tok)
        update(kn_ref[0, 0], vn_ref[0, 0], jnp.where(keep, 0.0, NEG_BIG))
        lam = _lambda_value(lq1_ref[...], lk1_ref[...], lq2_ref[...], lk2_ref[...], lambda_init)
        on = acc_ref[...] / jnp.tile(l_ref[...], (1, VALUE_DIM // LANES))
        for h in range(HEADS_PER_STEP):
            r0 = h * rows_per_head
            o = on[r0:r0 + n_tok, :] - lam * on[r0 + n_tok:r0 + 2 * n_tok, :]
            cols = slice(h * VALUE_DIM, (h + 1) * VALUE_DIM)
            o_ref[0, :, cols] = _subln_gate(o, sg_ref[...], lambda_init, sz_ref[0, :, cols]).astype(o_ref.dtype)


def _decode_attention(q, k_new, v_new, sz, cache_k, cache_v, page_off, page_table, subln_g, lam_vecs,
                      lambda_init, bsz, n_tok):
    n_pages = page_table.shape[1]
    pps = PAGES_PER_STEP
    hps = HEADS_PER_STEP
    n_hb = N_HEADS // hps
    assert n_pages % pps == 0 and n_tok * hps <= LANES
    rows = hps * 2 * n_tok
    q6 = q.reshape(bsz, n_tok, n_hb, hps, 2, HEAD_DIM)
    eye = jnp.eye(2, dtype=BF16)
    qm = jnp.einsum('btnhcd,ck->bnhctkd', q6, eye).reshape(bsz, n_hb, rows, VALUE_DIM)

    def new_rows(x):
        x = x.reshape(bsz, n_tok, n_hb, hps, VALUE_DIM).transpose(0, 2, 1, 3, 4)
        x = x.reshape(bsz, n_hb, n_tok * hps, VALUE_DIM)
        return jnp.pad(x, ((0, 0), (0, 0), (0, LANES - n_tok * hps), (0, 0)))

    def page_spec(r):
        return pl.BlockSpec((1, PAGE_SIZE, hps, VALUE_DIM),
                            lambda b, hb, s, pt: (page_off + pt[b, s * pps + r], 0, hb, 0))
    newspec = pl.BlockSpec((1, 1, LANES, VALUE_DIM), lambda b, hb, s, pt: (b, hb, 0, 0))
    tokspec = pl.BlockSpec((1, n_tok, hps * VALUE_DIM), lambda b, hb, s, pt: (b, 0, hb))
    vec = pl.BlockSpec((1, HEAD_DIM), lambda b, hb, s, pt: (0, 0))
    grid_spec = pltpu.PrefetchScalarGridSpec(
        num_scalar_prefetch=1,
        grid=(bsz, n_hb, n_pages // pps),
        in_specs=[pl.BlockSpec((1, 1, rows, VALUE_DIM), lambda b, hb, s, pt: (b, hb, 0, 0))]
                 + [page_spec(r) for r in range(pps)] + [page_spec(r) for r in range(pps)]
                 + [newspec, newspec, tokspec,
                    pl.BlockSpec((1, VALUE_DIM), lambda b, hb, s, pt: (0, 0)), vec, vec, vec, vec],
        out_specs=tokspec,
        scratch_shapes=[pltpu.VMEM((rows, LANES), F32), pltpu.VMEM((rows, LANES), F32),
                        pltpu.VMEM((rows, VALUE_DIM), F32),
                        pltpu.VMEM((rows, pps * PAGE_SIZE * hps), F32)],
    )
    out = pl.pallas_call(
        functools.partial(_decode_kernel, lambda_init, n_tok),
        grid_spec=grid_spec,
        out_shape=jax.ShapeDtypeStruct((bsz, n_tok, D_MODEL), BF16),
        compiler_params=_cparams(("parallel", "parallel", "arbitrary")),
        name="decode_diff_attention",
    )(page_table, qm, *([cache_k] * pps), *([cache_v] * pps),
      new_rows(k_new), new_rows(v_new),
      sz.reshape(bsz, n_tok, D_MODEL), subln_g.reshape(1, VALUE_DIM),
      *[x.reshape(1, HEAD_DIM) for x in lam_vecs])
    return out.reshape(bsz * n_tok, D_MODEL)


def _lambda_init(layer):
    return 0.8 - 0.6 * math.exp(-0.3 * layer)


def _s5_layer(hs2, xn2, idx, w_in, w_glu, b_glu, w_out, prep, d, h0, dims, seg_len):
    (u_p,), (u_s,) = _matmul(_Rows(xn2[0]), _Rows(xn2[1]), w_in, idx, 0, D_MODEL, _epi_f32, [F32], "s5_in_u")
    (sz_p,), (sz_s,) = _matmul(_Rows(xn2[0]), _Rows(xn2[1]), w_in, idx, D_MODEL, D_MODEL, _epi_silu, [F32],
                               "s5_in_z")
    gated, states = [], []
    for u, sz, (bsz, seq), h0ri, sl in ((u_p, sz_p, dims[0], (None, None), seg_len),
                                        (u_s, sz_s, dims[1], h0, dims[1][1])):
        g32, g16, hr, hi = _s5_scan(u.reshape(bsz, seq, D_MODEL), prep, d, h0ri[0], h0ri[1], sl)
        gated.append(_Rows(g16.reshape(bsz * seq, D_MODEL),
                           [(g32.reshape(bsz * seq, D_MODEL), "mn"), (sz, "mn"), (b_glu.reshape(1, D_MODEL), "n")]))
        shape = (bsz, S5_GROUPS, S5_STATE)
        states.append((hr.reshape(shape), hi.reshape(shape)))
    (m_p,), (m_s,) = _matmul(gated[0], gated[1], w_glu, idx, 0, D_MODEL, _epi_glu, [BF16], "s5_glu")
    (h_p,), (h_s,) = _matmul(_Rows(m_p, [(hs2[0], "mn")]), _Rows(m_s, [(hs2[1], "mn")]), w_out, idx, 0, D_MODEL,
                             _epi_residual, [F32], "s5_out")
    return (h_p, h_s), states


def _attn_in_proj(xn2, w_in, idx, k_stacks, v_stacks):
    big, small = _Rows(xn2[0]), _Rows(xn2[1])
    q = _matmul(big, small, w_in, idx, 0, D_MODEL, _epi_scaled_bf16, [BF16], "attn_in_q")
    k = _matmul(_Rows(xn2[0], stack=(idx, k_stacks[0])), _Rows(xn2[1], stack=(idx, k_stacks[1])), w_in, idx,
                D_MODEL, D_MODEL, _epi_f32_bf16, [F32, BF16], "attn_in_k")
    v = _matmul(_Rows(xn2[0], stack=(idx, v_stacks[0])), _Rows(xn2[1], stack=(idx, v_stacks[1])), w_in, idx,
                2 * D_MODEL, D_MODEL, _epi_f32_bf16, [F32, BF16], "attn_in_v")
    sz = _matmul(big, small, w_in, idx, 3 * D_MODEL, D_MODEL, _epi_silu, [F32], "attn_in_z")
    return q, k, v, sz


def kernel(x_prompt, x_sample, cache_k, cache_v, state_s5_re, state_s5_im, page_table, norm_g, final_norm_g, s5_w_in, s5_lambda_re, s5_lambda_im, s5_log_step, s5_b_re, s5_b_im, s5_c_re, s5_c_im, s5_d, s5_w_glu, s5_b_glu, s5_w_out, attn_w_in, attn_lam_q1, attn_lam_k1, attn_lam_q2, attn_lam_k2, attn_subln_g, attn_w_out):
    bp, lp, _ = x_prompt.shape
    bs, ls, _ = x_sample.shape
    depth = norm_g.shape[0]
    n_pool = cache_k.shape[1]
    seg_len = min(128, lp // 2)

    hp = x_prompt.reshape(bp * lp, D_MODEL)
    hs = x_sample.reshape(bs * ls, D_MODEL)
    n_attn = depth // 2
    k_p = jnp.zeros((n_attn, bp * lp, D_MODEL), F32)
    v_p = jnp.zeros((n_attn, bp * lp, D_MODEL), F32)
    k_s = jnp.zeros((n_attn, bs * ls, D_MODEL), F32)
    v_s = jnp.zeros((n_attn, bs * ls, D_MODEL), F32)
    sr_p, si_p, sr_s, si_s = [], [], [], []
    for layer in range(depth):
        idx = layer // 2
        xn2 = (_rmsnorm(hp, norm_g[layer], BF16), _rmsnorm(hs, norm_g[layer], BF16))
        if layer % 2 == 0:
            prep = _s5_prepare(s5_lambda_re[idx], s5_lambda_im[idx], s5_log_step[idx],
                               s5_b_re[idx], s5_b_im[idx], s5_c_re[idx], s5_c_im[idx], seg_len)
            d = s5_d[idx].reshape(1, D_MODEL)
            h0 = (state_s5_re[idx].reshape(bs, S5_GROUPS * S5_STATE),
                  state_s5_im[idx].reshape(bs, S5_GROUPS * S5_STATE))
            (hp, hs), states = _s5_layer((hp, hs), xn2, idx, s5_w_in, s5_w_glu, s5_b_glu[idx], s5_w_out,
                                         prep, d, h0, ((bp, lp), (bs, ls)), seg_len)
            sr_p.append(states[0][0]); si_p.append(states[0][1])
            sr_s.append(states[1][0]); si_s.append(states[1][1])
        else:
            lam0 = _lambda_init(layer)
            lam_vecs = (attn_lam_q1[idx], attn_lam_k1[idx], attn_lam_q2[idx], attn_lam_k2[idx])
            q, k, v, sz = _attn_in_proj(xn2, attn_w_in, idx, (k_p, k_s), (v_p, v_s))
            (k_p, k16_p), (k_s, k16_s) = k
            (v_p, v16_p), (v_s, v16_s) = v
            m_p = _flash_attention(q[0][0], k16_p, v16_p, sz[0][0], attn_subln_g[idx], lam_vecs, lam0, bp, lp)
            m_s = _decode_attention(q[1][0], k16_s, v16_s, sz[1][0],
                                    cache_k.reshape(-1, PAGE_SIZE, N_HEADS, VALUE_DIM),
                                    cache_v.reshape(-1, PAGE_SIZE, N_HEADS, VALUE_DIM),
                                    idx * n_pool, page_table, attn_subln_g[idx], lam_vecs, lam0, bs, ls)
            (hp,), (hs,) = _matmul(_Rows(m_p, [(hp, "mn")]), _Rows(m_s, [(hs, "mn")]), attn_w_out, idx, 0, D_MODEL,
                                   _epi_residual, [F32], "attn_out")
    y_prompt = _rmsnorm(hp, final_norm_g, F32).reshape(bp, lp, D_MODEL)
    y_sample = _rmsnorm(hs, final_norm_g, F32).reshape(bs, ls, D_MODEL)
    kv_p = (n_attn, bp, lp, N_HEADS, VALUE_DIM)
    kv_s = (n_attn, bs, ls, N_HEADS, VALUE_DIM)
    return (y_prompt, y_sample, k_p.reshape(kv_p), v_p.reshape(kv_p), k_s.reshape(kv_s), v_s.reshape(kv_s),
            jnp.stack(sr_p), jnp.stack(si_p), jnp.stack(sr_s), jnp.stack(si_s))
```
